```python
import math
import jax, jax.numpy as jnp
from jax import lax
import numpy as np

D_MODEL = 1024
BATCH = 32
SEQ = 2048
DEPTH = 2

MEM_LEN = 256
CONV_CH = 512
CONV_K = 31
SG_CH = 512
SG_GROUPS = 4
SG_CHUNK = 128
HEAD_DIM = 64
HEADS_PER_GROUP = 4
DIL_GROUPS = ((128, 1), (512, 4), (2048, 16))
ATT_HEADS = HEADS_PER_GROUP * len(DIL_GROUPS)
ATT_BLOCK = 128
REL_BUCKETS = 32
REL_MAX_DIST = 2048
N_BRANCH = 3
X_HEADS = 4
X_HEAD_DIM = D_MODEL // X_HEADS
D_FF = 2816
FFN_CONV_K = 3
NORM_EPS = 1e-6
LN_EPS = 1e-5

COL_A = 2 * CONV_CH
COL_B = 2 * SG_CH
COL_C = 3 * ATT_HEADS * HEAD_DIM
COL_G = N_BRANCH * D_MODEL
OFF_B = COL_A
OFF_C = COL_A + COL_B
OFF_G = COL_A + COL_B + COL_C
IN_COLS = COL_A + COL_B + COL_C + COL_G
ATT_OUT = HEADS_PER_GROUP * HEAD_DIM

kernel_name = "hybrid_conv_sgmlp_dilated_attn_block"


def rms_norm(x, g):
    xf = x.astype(jnp.float32)
    y = xf * lax.rsqrt(jnp.mean(xf * xf, axis=-1, keepdims=True) + NORM_EPS)
    return (y * g.astype(jnp.float32)).astype(x.dtype)


def layer_norm(x, g, b):
    xf = x.astype(jnp.float32)
    mu = jnp.mean(xf, axis=-1, keepdims=True)
    var = jnp.mean(jnp.square(xf - mu), axis=-1, keepdims=True)
    y = (xf - mu) * lax.rsqrt(var + LN_EPS)
    return (y * g.astype(jnp.float32) + b.astype(jnp.float32)).astype(x.dtype)


def causal_dwconv(x, w, b):
    k, c = w.shape
    y = lax.conv_general_dilated(x, w[:, None, :], window_strides=(1,), padding=[(k - 1, 0)],
                                 dimension_numbers=('NWC', 'WIO', 'NWC'), feature_group_count=c)
    return y + b


def t5_bucket(dist):
    n = jnp.maximum(dist, 0)
    max_exact = REL_BUCKETS // 2
    nf = jnp.maximum(n, 1).astype(jnp.float32)
    large = max_exact + (jnp.log(nf / max_exact) / math.log(REL_MAX_DIST / max_exact)
                         * (REL_BUCKETS - max_exact)).astype(jnp.int32)
    large = jnp.minimum(large, REL_BUCKETS - 1)
    return jnp.where(n < max_exact, n, large)


def dilated_window_attention(q, k, v, rel_table, dilation, window):
    B, S, H, E = q.shape
    span = window // dilation
    unit = dilation * ATT_BLOCK
    s_pad = -(-S // unit) * unit
    L = s_pad // dilation
    nb = L // ATT_BLOCK

    def to_blocks(t):
        t = jnp.pad(t, ((0, 0), (0, s_pad - S), (0, 0), (0, 0)))
        t = t.reshape(B, L, dilation, H, E).swapaxes(1, 2)
        return t.reshape(B, dilation, nb, ATT_BLOCK, H, E)

    def with_prev(t):
        prev = jnp.pad(t, ((0, 0), (0, 0), (1, 0), (0, 0), (0, 0), (0, 0)))[:, :, :-1]
        return jnp.concatenate([prev, t], axis=3)

    qb = to_blocks(q)
    kk = with_prev(to_blocks(k))
    vv = with_prev(to_blocks(v))

    qi = jnp.arange(ATT_BLOCK)[:, None]
    ki = jnp.arange(2 * ATT_BLOCK)[None, :]
    rel = qi + ATT_BLOCK - ki
    bias = rel_table[t5_bucket(rel * dilation)].transpose(2, 0, 1).astype(jnp.float32)
    blk_idx = jnp.arange(nb)[:, None, None]
    valid = (rel >= 0) & (rel <= span) & ((blk_idx > 0) | (ki >= ATT_BLOCK))

    scores = jnp.einsum('bcnqhe,bcnkhe->bcnhqk', qb, kk).astype(jnp.float32) * (E ** -0.5) + bias
    scores = jnp.where(valid[:, None], scores, -jnp.inf)
    m = jnp.max(scores, axis=-1, keepdims=True)
    e = jnp.exp(scores - m)
    s = jnp.sum(e, axis=-1, keepdims=True)
    o = jnp.einsum('bcnhqk,bcnkhe->bcnqhe', e, vv.astype(jnp.float32)) / s.swapaxes(3, 4)
    lse = (m + jnp.log(s))[..., 0].swapaxes(3, 4)

    def from_blocks(t):
        tail = t.shape[4:]
        t = t.reshape(B, dilation, L, *tail).swapaxes(1, 2).reshape(B, s_pad, *tail)
        return t[:, :S]

    return from_blocks(o), from_blocks(lse)


def parallel_mixer(h, w_in, b_gate, conv_a_w, conv_a_b, ln_a_g, ln_a_b, w_a_out,
                   ln_b_g, ln_b_b, w_s, b_s, w_b_out, rel_bias, w_c_out, w_mix_out):
    B, S, _ = h.shape
    za = h @ w_in[:, :COL_A]
    a_val, a_gate = jnp.split(za, 2, axis=-1)
    a = a_val * jax.nn.sigmoid(a_gate)
    a = causal_dwconv(a, conv_a_w, conv_a_b)
    a = jax.nn.silu(layer_norm(a, ln_a_g, ln_a_b))
    y_a = a @ w_a_out
    zb = jax.nn.gelu(h @ w_in[:, OFF_B:OFF_C], approximate=True)
    u, v = jnp.split(zb, 2, axis=-1)
    v = layer_norm(v, ln_b_g, ln_b_b)
    v = v.reshape(B, S // SG_CHUNK, SG_CHUNK, SG_GROUPS, SG_CH // SG_GROUPS)
    causal = jnp.tril(jnp.ones((SG_CHUNK, SG_CHUNK), dtype=bool))
    ws = jnp.where(causal, w_s, 0.0)
    v = jnp.einsum('gts,bnsgc->bntgc', ws, v) + b_s.T[:, :, None]
    y_b = (u * v.reshape(B, S, SG_CH)) @ w_b_out
    zc = h @ w_in[:, OFF_C:OFF_G]
    q, k, vc = [t.reshape(B, S, ATT_HEADS, HEAD_DIM) for t in jnp.split(zc, 3, axis=-1)]
    outs, lses = [], []
    for gi, (window, dil) in enumerate(DIL_GROUPS):
        sl = slice(gi * HEADS_PER_GROUP, (gi + 1) * HEADS_PER_GROUP)
        o, l = dilated_window_attention(q[:, :, sl], k[:, :, sl], vc[:, :, sl], rel_bias[:, sl], dil, window)
        outs.append(o)
        lses.append(l)
    wts = jax.nn.softmax(jnp.stack(lses, axis=0), axis=0)
    oc = jnp.sum(wts[..., None] * jnp.stack(outs, axis=0), axis=0)
    y_c = oc.reshape(B, S, ATT_OUT).astype(h.dtype) @ w_c_out
    zg = (h @ w_in[:, OFF_G:]).reshape(B, S, N_BRANCH, D_MODEL) + b_gate
    g = jax.nn.sigmoid(zg)
    merged = g[:, :, 0] * y_a + g[:, :, 1] * y_b + g[:, :, 2] * y_c
    return merged @ w_mix_out


def memory_cross_attention(h, mem_n, w_xq, w_xkv, w_xo):
    B, S, _ = h.shape
    M = mem_n.shape[1]
    q = (h @ w_xq).reshape(B, S, X_HEADS, X_HEAD_DIM)
    k, v = [t.reshape(B, M, X_HEADS, X_HEAD_DIM) for t in jnp.split(mem_n @ w_xkv, 2, axis=-1)]
    s = jnp.einsum('bshe,bmhe->bhsm', q, k).astype(jnp.float32) * (X_HEAD_DIM ** -0.5)
    p = jax.nn.softmax(s, axis=-1)
    o = jnp.einsum('bhsm,bmhe->bshe', p, v.astype(jnp.float32)).reshape(B, S, D_MODEL)
    return o.astype(h.dtype) @ w_xo


def conv_ffn(h, w_up, conv_f_w, conv_f_b, w_down):
    gate = causal_dwconv(h @ w_up[:, :D_FF], conv_f_w, conv_f_b)
    val = h @ w_up[:, D_FF:]
    return (jax.nn.gelu(gate, approximate=True) * val) @ w_down


def setup_inputs(seed: int = 0) -> dict:
    key = jax.random.key(seed)
    ks = iter(jax.random.split(key, 40))

    def nrm(shape, scale):
        return jax.random.normal(next(ks), shape, jnp.float32) * scale

    def gain(shape):
        return 1.0 + nrm(shape, 0.05)

    L = DEPTH
    return {
        "x": nrm((BATCH, SEQ, D_MODEL), 1.0),
        "mem": nrm((BATCH, MEM_LEN, D_MODEL), 1.0),
        "rel_bias": nrm((REL_BUCKETS, ATT_HEADS), 0.5),
        "mix_pre_g": gain((L, D_MODEL)),
        "mix_post_g": gain((L, D_MODEL)),
        "w_in": nrm((L, D_MODEL, IN_COLS), D_MODEL ** -0.5),
        "b_gate": nrm((L, N_BRANCH, D_MODEL), 0.02),
        "conv_a_w": nrm((L, CONV_K, CONV_CH), CONV_K ** -0.5),
        "conv_a_b": nrm((L, CONV_CH), 0.02),
        "ln_a_g": gain((L, CONV_CH)),
        "ln_a_b": nrm((L, CONV_CH), 0.02),
        "w_a_out": nrm((L, CONV_CH, D_MODEL), CONV_CH ** -0.5),
        "ln_b_g": gain((L, SG_CH)),
        "ln_b_b": nrm((L, SG_CH), 0.02),
        "w_s": nrm((L, SG_GROUPS, SG_CHUNK, SG_CHUNK), SG_CHUNK ** -0.5),
        "b_s": 1.0 + nrm((L, SG_GROUPS, SG_CHUNK), 0.05),
        "w_b_out": nrm((L, SG_CH, D_MODEL), SG_CH ** -0.5),
        "w_c_out": nrm((L, ATT_OUT, D_MODEL), ATT_OUT ** -0.5),
        "w_mix_out": nrm((L, D_MODEL, D_MODEL), D_MODEL ** -0.5),
        "x_pre_g": gain((L, D_MODEL)),
        "x_post_g": gain((L, D_MODEL)),
        "mem_g": gain((L, D_MODEL)),
        "w_xq": nrm((L, D_MODEL, D_MODEL), D_MODEL ** -0.5),
        "w_xkv": nrm((L, D_MODEL, 2 * D_MODEL), D_MODEL ** -0.5),
        "w_xo": nrm((L, D_MODEL, D_MODEL), D_MODEL ** -0.5),
        "ffn_pre_g": gain((L, D_MODEL)),
        "ffn_post_g": gain((L, D_MODEL)),
        "w_up": nrm((L, D_MODEL, 2 * D_FF), D_MODEL ** -0.5),
        "conv_f_w": nrm((L, FFN_CONV_K, D_FF), FFN_CONV_K ** -0.5),
        "conv_f_b": nrm((L, D_FF), 0.02),
        "w_down": nrm((L, D_FF, D_MODEL), D_FF ** -0.5),
    }


def reference(x, mem, rel_bias, mix_pre_g, mix_post_g, w_in, b_gate, conv_a_w, conv_a_b,
              ln_a_g, ln_a_b, w_a_out, ln_b_g, ln_b_b, w_s, b_s, w_b_out, w_c_out, w_mix_out,
              x_pre_g, x_post_g, mem_g, w_xq, w_xkv, w_xo,
              ffn_pre_g, ffn_post_g, w_up, conv_f_w, conv_f_b, w_down):
    for l in range(DEPTH):
        h = rms_norm(x, mix_pre_g[l])
        y = parallel_mixer(h, w_in[l], b_gate[l], conv_a_w[l], conv_a_b[l], ln_a_g[l], ln_a_b[l],
                           w_a_out[l], ln_b_g[l], ln_b_b[l], w_s[l], b_s[l], w_b_out[l],
                           rel_bias, w_c_out[l], w_mix_out[l])
        x = x + rms_norm(y, mix_post_g[l])

        h = rms_norm(x, x_pre_g[l])
        y = memory_cross_attention(h, rms_norm(mem, mem_g[l]), w_xq[l], w_xkv[l], w_xo[l])
        x = x + rms_norm(y, x_post_g[l])

        h = rms_norm(x, ffn_pre_g[l])
        y = conv_ffn(h, w_up[l], conv_f_w[l], conv_f_b[l], w_down[l])
        x = x + rms_norm(y, ffn_post_g[l])
    return x
```

```python
import functools
import math

import numpy as np
import jax
import jax.numpy as jnp
from jax import lax
from jax.experimental import pallas as pl
from jax.experimental.pallas import tpu as pltpu

D_MODEL = 1024
SEQ = 2048
MEM_LEN = 256
CONV_CH = 512
CONV_K = 31
SG_CH = 512
SG_GROUPS = 4
SG_CHUNK = 128
HEAD_DIM = 64
HEADS_PER_GROUP = 4
DIL_GROUPS = ((128, 1), (512, 4), (2048, 16))
ATT_HEADS = HEADS_PER_GROUP * len(DIL_GROUPS)
ATT_BLOCK = 128
REL_BUCKETS = 32
REL_MAX_DIST = 2048
N_BRANCH = 3
X_HEADS = 4
X_HEAD_DIM = D_MODEL // X_HEADS
D_FF = 2816
FFN_CONV_K = 3
NORM_EPS = 1e-6
LN_EPS = 1e-5

COL_A = 2 * CONV_CH
COL_B = 2 * SG_CH
COL_C = 3 * ATT_HEADS * HEAD_DIM
OFF_B = COL_A
OFF_C = COL_A + COL_B
OFF_G = COL_A + COL_B + COL_C
ATT_OUT = HEADS_PER_GROUP * HEAD_DIM

LANES = 128
SUBLANES = 8
VMEM_LIMIT_BYTES = 56 * 1024 * 1024

PROJ_TILE = 512
SEQ_TILE = 512
CONV_ROWS = 64
CONV_HALO = 32
FFN_HALO = 8

BF16 = jnp.bfloat16
F32 = jnp.float32
_NT = (((1,), (1,)), ((), ()))


def _rms(x, g):
    return x * lax.rsqrt(jnp.mean(x * x, axis=-1, keepdims=True) + NORM_EPS) * g


def _layer_norm(x, g, b):
    mu = jnp.mean(x, axis=-1, keepdims=True)
    xc = x - mu
    var = jnp.mean(xc * xc, axis=-1, keepdims=True)
    return xc * lax.rsqrt(var + LN_EPS) * g + b


def _sigmoid(x):
    return 1.0 / (1.0 + jnp.exp(-x))


def _gelu_tanh(x):
    c = math.sqrt(2.0 / math.pi)
    return 0.5 * x * (1.0 + jnp.tanh(c * (x + 0.044715 * (x * x * x))))


def _dot(a, b):
    return jnp.dot(a, b, preferred_element_type=F32)


def _const_spec(shape):
    n = len(shape)
    return pl.BlockSpec(shape, lambda *_: (0,) * n, pipeline_mode=pl.Buffered(1))


def _bucket_tables():
    qi = np.arange(ATT_BLOCK)[:, None]
    ki = np.arange(2 * ATT_BLOCK)[None, :]
    rel = qi + ATT_BLOCK - ki
    max_exact = REL_BUCKETS // 2
    out = []
    for window, dil in DIL_GROUPS:
        span = window // dil
        dist = np.maximum(rel * dil, 0)
        nf = np.maximum(dist, 1)
        vals = []
        for dt in (np.float32, np.float64):
            x = np.log(nf.astype(dt) / dt(max_exact)) / dt(math.log(REL_MAX_DIST / max_exact))
            vals.append(max_exact + (x * dt(REL_BUCKETS - max_exact)).astype(np.int32))
        valid = (rel >= 0) & (rel <= span)
        assert np.array_equal(vals[0][valid], vals[1][valid])
        large = np.minimum(vals[0], REL_BUCKETS - 1)
        bucket = np.where(dist < max_exact, dist, large)
        out.append(np.where(valid, bucket, -1).astype(np.int32))
    return np.stack(out)


def _bias_kernel(tab_ref, bucket_ref, out_ref):
    for gi in range(len(DIL_GROUPS)):
        bk = bucket_ref[gi]
        for hh in range(HEADS_PER_GROUP):
            h = gi * HEADS_PER_GROUP + hh
            acc = jnp.full(bk.shape, -jnp.inf, F32)
            for b in range(REL_BUCKETS):
                acc = jnp.where(bk == b, tab_ref[b, h], acc)
            out_ref[h] = acc


def _bias_tables(rel_bias):
    buckets = jnp.asarray(_bucket_tables())
    return pl.pallas_call(
        _bias_kernel,
        out_shape=jax.ShapeDtypeStruct((ATT_HEADS, ATT_BLOCK, 2 * ATT_BLOCK), F32),
        in_specs=[pl.BlockSpec(memory_space=pltpu.SMEM),
                  pl.BlockSpec(memory_space=pltpu.VMEM)],
        out_specs=pl.BlockSpec(memory_space=pltpu.VMEM),
        name="rel_bias_tables",
    )(rel_bias, buckets)


def _attend_pair(q2, k2, v2, bias_a, bias_b):
    lane = lax.broadcasted_iota(jnp.int32, (ATT_BLOCK, LANES), 1)
    lo = lane < HEAD_DIM
    zero = jnp.zeros_like(q2)

    def one(qm, bias):
        s = lax.dot_general(qm, k2, _NT, preferred_element_type=F32) + bias
        m = jnp.max(s, axis=-1, keepdims=True)
        e = jnp.exp(s - m)
        l = jnp.sum(e, axis=-1, keepdims=True)
        return m, l, _dot(e.astype(BF16), v2)

    ma, la, aa = one(jnp.where(lo, q2, zero), bias_a)
    mb, lb, ab = one(jnp.where(lo, zero, q2), bias_b)
    shape = (ATT_BLOCK, LANES)
    m2 = jnp.where(lo, jnp.broadcast_to(ma, shape), jnp.broadcast_to(mb, shape))
    l2 = jnp.where(lo, jnp.broadcast_to(la, shape), jnp.broadcast_to(lb, shape))
    a2 = jnp.where(lo, aa, ab)
    return m2, l2, a2


def _attn_kernel(x_ref, g_ref, w_ref, bias_ref, oc_ref,
                 q_s, k_s, v_s, stage, m_s, l_s, acc_s):
    g = g_ref[...]
    hw = ATT_OUT

    def proj_tile(tt, carry):
        r0 = pl.multiple_of(tt * PROJ_TILE, PROJ_TILE)
        h = _rms(x_ref[pl.ds(r0, PROJ_TILE), :], g).astype(BF16)
        for gi, (_, dil) in enumerate(DIL_GROUPS):
            z = _dot(h, w_ref[gi])
            if dil == 1:
                q_s[gi, pl.ds(r0, PROJ_TILE), :] = (z[:, :hw] * HEAD_DIM ** -0.5).astype(BF16)
                k_s[gi, pl.ds(r0, PROJ_TILE), :] = z[:, hw:2 * hw].astype(BF16)
                v_s[gi, pl.ds(r0, PROJ_TILE), :] = z[:, 2 * hw:].astype(BF16)
            else:
                nt = hw // LANES
                for j in range(3 * nt):
                    stage[j] = z[:, j * LANES:(j + 1) * LANES]
                n = PROJ_TILE // dil
                cls_len = SEQ // dil
                for c in range(dil):
                    dst = pl.multiple_of(c * cls_len + tt * n, n)
                    for j in range(3 * nt):
                        zz = stage[j, pl.ds(c, n, stride=dil), :]
                        cols = slice((j % nt) * LANES, (j % nt + 1) * LANES)
                        if j < nt:
                            q_s[gi, pl.ds(dst, n), cols] = (zz * HEAD_DIM ** -0.5).astype(BF16)
                        elif j < 2 * nt:
                            k_s[gi, pl.ds(dst, n), cols] = zz.astype(BF16)
                        else:
                            v_s[gi, pl.ds(dst, n), cols] = zz.astype(BF16)
        return carry

    lax.fori_loop(0, SEQ // PROJ_TILE, proj_tile, 0)

    def block(gi, q_row, k_row, nk, out_rows, mode):
        for pr in range(HEADS_PER_GROUP // 2):
            cols = slice(pr * LANES, (pr + 1) * LANES)
            q2 = q_s[gi, pl.ds(q_row, ATT_BLOCK), cols]
            k2 = k_s[gi, pl.ds(k_row, nk), cols]
            v2 = v_s[gi, pl.ds(k_row, nk), cols]
            ha = gi * HEADS_PER_GROUP + 2 * pr
            if nk == ATT_BLOCK:
                ba = bias_ref[ha, :, ATT_BLOCK:]
                bb = bias_ref[ha + 1, :, ATT_BLOCK:]
            else:
                ba = bias_ref[ha]
                bb = bias_ref[ha + 1]
            m2, l2, a2 = _attend_pair(q2, k2, v2, ba, bb)
            if mode == "init":
                m_s[pr, out_rows, :] = m2
                l_s[pr, out_rows, :] = l2
                acc_s[pr, out_rows, :] = a2
            else:
                mo = m_s[pr, out_rows, :]
                mn = jnp.maximum(mo, m2)
                eo = jnp.exp(mo - mn)
                e2 = jnp.exp(m2 - mn)
                ln = eo * l_s[pr, out_rows, :] + e2 * l2
                an = eo * acc_s[pr, out_rows, :] + e2 * a2
                if mode == "merge":
                    m_s[pr, out_rows, :] = mn
                    l_s[pr, out_rows, :] = ln
                    acc_s[pr, out_rows, :] = an
                else:
                    oc_ref[out_rows, cols] = (an / ln).astype(BF16)

    dil2 = DIL_GROUPS[2][1]

    def g2_body(c, carry):
        row = pl.multiple_of(c * ATT_BLOCK, ATT_BLOCK)
        block(2, row, row, ATT_BLOCK, pl.ds(c, ATT_BLOCK, stride=dil2), "init")
        return carry

    lax.fori_loop(0, dil2, g2_body, 0)

    dil1 = DIL_GROUPS[1][1]
    nb1 = SEQ // dil1 // ATT_BLOCK

    def g1_body(i, carry):
        c = i // nb1
        n = i % nb1
        row = pl.multiple_of(i * ATT_BLOCK, ATT_BLOCK)
        out_rows = pl.ds(n * (ATT_BLOCK * dil1) + c, ATT_BLOCK, stride=dil1)

        @pl.when(n == 0)
        def _():
            block(1, row, row, ATT_BLOCK, out_rows, "merge")

        @pl.when(n > 0)
        def _():
            prev = pl.multiple_of(row - ATT_BLOCK, ATT_BLOCK)
            block(1, row, prev, 2 * ATT_BLOCK, out_rows, "merge")

        return carry

    lax.fori_loop(0, dil1 * nb1, g1_body, 0)

    block(0, 0, 0, ATT_BLOCK, pl.ds(0, ATT_BLOCK), "final")

    def g0_body(n, carry):
        row = pl.multiple_of(n * ATT_BLOCK, ATT_BLOCK)
        prev = pl.multiple_of(row - ATT_BLOCK, ATT_BLOCK)
        block(0, row, prev, 2 * ATT_BLOCK, pl.ds(row, ATT_BLOCK), "final")
        return carry

    lax.fori_loop(1, SEQ // ATT_BLOCK, g0_body, 0)


def _attention(x, pre_g, w_qkv, bias):
    B = x.shape[0]
    ng = len(DIL_GROUPS)
    return pl.pallas_call(
        _attn_kernel,
        out_shape=jax.ShapeDtypeStruct((B, SEQ, ATT_OUT), BF16),
        grid=(B,),
        in_specs=[
            pl.BlockSpec((None, SEQ, D_MODEL), lambda b: (b, 0, 0)),
            _const_spec((1, D_MODEL)),
            _const_spec((ng, D_MODEL, 3 * ATT_OUT)),
            _const_spec((ATT_HEADS, ATT_BLOCK, 2 * ATT_BLOCK)),
        ],
        out_specs=pl.BlockSpec((None, SEQ, ATT_OUT), lambda b: (b, 0, 0)),
        scratch_shapes=[
            pltpu.VMEM((ng, SEQ, ATT_OUT), BF16),
            pltpu.VMEM((ng, SEQ, ATT_OUT), BF16),
            pltpu.VMEM((ng, SEQ, ATT_OUT), BF16),
            pltpu.VMEM((3 * ATT_OUT // LANES, PROJ_TILE, LANES), F32),
            pltpu.VMEM((ATT_OUT // LANES, SEQ, LANES), F32),
            pltpu.VMEM((ATT_OUT // LANES, SEQ, LANES), F32),
            pltpu.VMEM((ATT_OUT // LANES, SEQ, LANES), F32),
        ],
        compiler_params=pltpu.CompilerParams(
            dimension_semantics=("arbitrary",), vmem_limit_bytes=VMEM_LIMIT_BYTES),
        name="dilated_attention",
    )(x, pre_g, w_qkv, bias)


def _mixer_kernel(x_ref, oc_ref, pre_g, post_g, w_a, w_b, w_g, b_gate,
                  conv_w, conv_b, ln_a_g, ln_a_b, w_a_out,
                  ln_b_g, ln_b_b, ws_ref, bs_ref, w_b_out, w_c_out, w_mix,
                  out_ref, abuf, a2buf):
    ts = SEQ_TILE
    s = pl.program_id(1)
    x = x_ref[...]
    h = _rms(x, pre_g[...]).astype(BF16)

    za = _dot(h, w_a[...])
    a = za[:, :CONV_CH] * _sigmoid(za[:, CONV_CH:])

    @pl.when(s == 0)
    def _():
        abuf[0:CONV_HALO, :] = jnp.zeros((CONV_HALO, CONV_CH), F32)

    @pl.when(s > 0)
    def _():
        abuf[0:CONV_HALO, :] = abuf[ts:ts + CONV_HALO, :]

    abuf[CONV_HALO:CONV_HALO + ts, :] = a
    lag = CONV_HALO - (CONV_K - 1)
    for rc in range(ts // CONV_ROWS):
        base = rc * CONV_ROWS + lag
        acc = jnp.broadcast_to(conv_b[...], (CONV_ROWS, CONV_CH))
        for j in range(CONV_K):
            acc = acc + conv_w[j:j + 1, :] * abuf[base + j:base + j + CONV_ROWS, :]
        y = _layer_norm(acc, ln_a_g[...], ln_a_b[...])
        a2buf[rc * CONV_ROWS:(rc + 1) * CONV_ROWS, :] = (y * _sigmoid(y)).astype(BF16)
    y_a = _dot(a2buf[...], w_a_out[...])
    merged = _sigmoid(_dot(h, w_g[:, 0:D_MODEL]) + b_gate[0:1, :]) * y_a

    zb = _gelu_tanh(_dot(h, w_b[...]))
    u = zb[:, :SG_CH]
    v = _layer_norm(zb[:, SG_CH:], ln_b_g[...], ln_b_b[...]).astype(BF16)
    nch = ts // SG_CHUNK
    gw = SG_CH // SG_GROUPS
    ti = lax.broadcasted_iota(jnp.int32, (SG_CHUNK, SG_CHUNK), 0)
    si = lax.broadcasted_iota(jnp.int32, (SG_CHUNK, SG_CHUNK), 1)
    causal = si <= ti
    mixed = []
    for gi in range(SG_GROUPS):
        wsm = jnp.where(causal, ws_ref[gi], 0.0).astype(BF16)
        rhs = jnp.concatenate(
            [v[n * SG_CHUNK:(n + 1) * SG_CHUNK, gi * gw:(gi + 1) * gw] for n in range(nch)], axis=1)
        r = _dot(wsm, rhs)
        mixed.append(jnp.concatenate(
            [r[:, n * gw:(n + 1) * gw] for n in range(nch)], axis=0))
    bs_tile = jnp.concatenate([bs_ref[...]] * nch, axis=0)
    vb = jnp.concatenate(mixed, axis=1) + bs_tile
    y_b = _dot((u * vb).astype(BF16), w_b_out[...])
    merged = merged + _sigmoid(_dot(h, w_g[:, D_MODEL:2 * D_MODEL]) + b_gate[1:2, :]) * y_b

    y_c = _dot(oc_ref[...], w_c_out[...])
    merged = merged + _sigmoid(_dot(h, w_g[:, 2 * D_MODEL:]) + b_gate[2:3, :]) * y_c

    y = _dot(merged.astype(BF16), w_mix[...])
    out_ref[...] = x + _rms(y, post_g[...])


def _mixer(x, oc, pre_g, post_g, w_a, w_b, w_g, b_gate, conv_w, conv_b, ln_a_g, ln_a_b, w_a_out,
           ln_b_g, ln_b_b, w_s, bs_cols, w_b_out, w_c_out, w_mix):
    B = x.shape[0]
    ts = SEQ_TILE
    consts = (pre_g, post_g, w_a, w_b, w_g, b_gate, conv_w, conv_b, ln_a_g, ln_a_b, w_a_out,
              ln_b_g, ln_b_b, w_s, bs_cols, w_b_out, w_c_out, w_mix)
    return pl.pallas_call(
        _mixer_kernel,
        out_shape=jax.ShapeDtypeStruct(x.shape, F32),
        grid=(B, SEQ // ts),
        in_specs=[
            pl.BlockSpec((None, ts, D_MODEL), lambda b, s: (b, s, 0)),
            pl.BlockSpec((None, ts, ATT_OUT), lambda b, s: (b, s, 0)),
        ] + [_const_spec(c.shape) for c in consts],
        out_specs=pl.BlockSpec((None, ts, D_MODEL), lambda b, s: (b, s, 0)),
        scratch_shapes=[
            pltpu.VMEM((CONV_HALO + ts, CONV_CH), F32),
            pltpu.VMEM((ts, CONV_CH), BF16),
        ],
        compiler_params=pltpu.CompilerParams(
            dimension_semantics=("arbitrary", "arbitrary"), vmem_limit_bytes=VMEM_LIMIT_BYTES),
        name="parallel_mixer",
    )(x, oc, *consts)


def _xattn_kernel(x_ref, mem_ref, pre_g, post_g, mem_g, w_q, w_kv, w_o, out_ref, k_s, v_s):
    s = pl.program_id(1)

    @pl.when(s == 0)
    def _():
        mn = _rms(mem_ref[...], mem_g[...]).astype(BF16)
        kv = _dot(mn, w_kv[...])
        k_s[...] = kv[:, :D_MODEL].astype(BF16)
        v_s[...] = kv[:, D_MODEL:].astype(BF16)

    x = x_ref[...]
    h = _rms(x, pre_g[...]).astype(BF16)
    q = (_dot(h, w_q[...]) * X_HEAD_DIM ** -0.5).astype(BF16)
    outs = []
    for hd in range(X_HEADS):
        cols = slice(hd * X_HEAD_DIM, (hd + 1) * X_HEAD_DIM)
        sc = lax.dot_general(q[:, cols], k_s[:, cols], _NT, preferred_element_type=F32)
        m = jnp.max(sc, axis=-1, keepdims=True)
        e = jnp.exp(sc - m)
        p = e / jnp.sum(e, axis=-1, keepdims=True)
        outs.append(_dot(p.astype(BF16), v_s[:, cols]))
    o = jnp.concatenate(outs, axis=1).astype(BF16)
    y = _dot(o, w_o[...])
    out_ref[...] = x + _rms(y, post_g[...])


def _cross_attention(x, mem, pre_g, post_g, mem_g, w_q, w_kv, w_o):
    B = x.shape[0]
    ts = SEQ_TILE
    consts = (pre_g, post_g, mem_g, w_q, w_kv, w_o)
    return pl.pallas_call(
        _xattn_kernel,
        out_shape=jax.ShapeDtypeStruct(x.shape, F32),
        grid=(B, SEQ // ts),
        in_specs=[
            pl.BlockSpec((None, ts, D_MODEL), lambda b, s: (b, s, 0)),
            pl.BlockSpec((None, MEM_LEN, D_MODEL), lambda b, s: (b, 0, 0)),
        ] + [_const_spec(c.shape) for c in consts],
        out_specs=pl.BlockSpec((None, ts, D_MODEL), lambda b, s: (b, s, 0)),
        scratch_shapes=[
            pltpu.VMEM((MEM_LEN, D_MODEL), BF16),
            pltpu.VMEM((MEM_LEN, D_MODEL), BF16),
        ],
        compiler_params=pltpu.CompilerParams(
            dimension_semantics=("arbitrary", "arbitrary"), vmem_limit_bytes=VMEM_LIMIT_BYTES),
        name="memory_cross_attention",
    )(x, mem, *consts)


def _ffn_kernel(x_ref, pre_g, post_g, w_gate, w_val, conv_w, conv_b, w_down, out_ref, gbuf):
    ts = SEQ_TILE
    s = pl.program_id(1)
    x = x_ref[...]
    h = _rms(x, pre_g[...]).astype(BF16)

    @pl.when(s == 0)
    def _():
        gbuf[0:FFN_HALO, :] = jnp.zeros((FFN_HALO, D_FF), F32)

    @pl.when(s > 0)
    def _():
        gbuf[0:FFN_HALO, :] = gbuf[ts:ts + FFN_HALO, :]

    gbuf[FFN_HALO:FFN_HALO + ts, :] = _dot(h, w_gate[...])
    lag = FFN_HALO - (FFN_CONV_K - 1)
    gc = jnp.broadcast_to(conv_b[...], (ts, D_FF))
    for j in range(FFN_CONV_K):
        gc = gc + conv_w[j:j + 1, :] * gbuf[lag + j:lag + j + ts, :]
    act = _gelu_tanh(gc) * _dot(h, w_val[...])
    y = _dot(act.astype(BF16), w_down[...])
    out_ref[...] = x + _rms(y, post_g[...])


def _conv_ffn(x, pre_g, post_g, w_gate, w_val, conv_w, conv_b, w_down):
    B = x.shape[0]
    ts = SEQ_TILE
    consts = (pre_g, post_g, w_gate, w_val, conv_w, conv_b, w_down)
    return pl.pallas_call(
        _ffn_kernel,
        out_shape=jax.ShapeDtypeStruct(x.shape, F32),
        grid=(B, SEQ // ts),
        in_specs=[pl.BlockSpec((None, ts, D_MODEL), lambda b, s: (b, s, 0))]
        + [_const_spec(c.shape) for c in consts],
        out_specs=pl.BlockSpec((None, ts, D_MODEL), lambda b, s: (b, s, 0)),
        scratch_shapes=[pltpu.VMEM((FFN_HALO + ts, D_FF), F32)],
        compiler_params=pltpu.CompilerParams(
            dimension_semantics=("arbitrary", "arbitrary"), vmem_limit_bytes=VMEM_LIMIT_BYTES),
        name="conv_ffn",
    )(x, *consts)


def _row(v):
    return v.reshape(1, -1)


def _qkv_weights(w_c):
    n = ATT_HEADS * HEAD_DIM
    parts = []
    for gi in range(len(DIL_GROUPS)):
        cols = [w_c[:, t * n + gi * ATT_OUT:t * n + (gi + 1) * ATT_OUT] for t in range(3)]
        parts.append(jnp.concatenate(cols, axis=1))
    return jnp.stack(parts).astype(BF16)


def kernel(x, mem, rel_bias, mix_pre_g, mix_post_g, w_in, b_gate, conv_a_w, conv_a_b, ln_a_g, ln_a_b, w_a_out, ln_b_g, ln_b_b, w_s, b_s, w_b_out, w_c_out, w_mix_out, x_pre_g, x_post_g, mem_g, w_xq, w_xkv, w_xo, ffn_pre_g, ffn_post_g, w_up, conv_f_w, conv_f_b, w_down):
    assert x.shape[1:] == (SEQ, D_MODEL) and mem.shape[1:] == (MEM_LEN, D_MODEL)
    depth = w_in.shape[0]
    bias = _bias_tables(rel_bias)
    gw = SG_CH // SG_GROUPS
    for l in range(depth):
        w = w_in[l]
        oc = _attention(x, _row(mix_pre_g[l]), _qkv_weights(w[:, OFF_C:OFF_G]), bias)
        bs_cols = jnp.repeat(b_s[l].T, gw, axis=1)
        x = _mixer(
            x, oc, _row(mix_pre_g[l]), _row(mix_post_g[l]),
            w[:, :OFF_B].astype(BF16), w[:, OFF_B:OFF_C].astype(BF16), w[:, OFF_G:].astype(BF16),
            b_gate[l], conv_a_w[l], _row(conv_a_b[l]), _row(ln_a_g[l]), _row(ln_a_b[l]),
            w_a_out[l].astype(BF16), _row(ln_b_g[l]), _row(ln_b_b[l]), w_s[l], bs_cols,
            w_b_out[l].astype(BF16), w_c_out[l].astype(BF16), w_mix_out[l].astype(BF16))
        x = _cross_attention(
            x, mem, _row(x_pre_g[l]), _row(x_post_g[l]), _row(mem_g[l]),
            w_xq[l].astype(BF16), w_xkv[l].astype(BF16), w_xo[l].astype(BF16))
        x = _conv_ffn(
            x, _row(ffn_pre_g[l]), _row(ffn_post_g[l]),
            w_up[l][:, :D_FF].astype(BF16), w_up[l][:, D_FF:].astype(BF16),
            conv_f_w[l], _row(conv_f_b[l]), w_down[l].astype(BF16))
    return x
```

```python
import functools
import math

import numpy as np
import jax
import jax.numpy as jnp
from jax import lax
from jax.experimental import pallas as pl
from jax.experimental.pallas import tpu as pltpu

D_MODEL = 1024
SEQ = 2048
MEM_LEN = 256
CONV_CH = 512
CONV_K = 31
SG_CH = 512
SG_GROUPS = 4
SG_CHUNK = 128
HEAD_DIM = 64
HEADS_PER_GROUP = 4
DIL_GROUPS = ((128, 1), (512, 4), (2048, 16))
ATT_HEADS = HEADS_PER_GROUP * len(DIL_GROUPS)
ATT_BLOCK = 128
REL_BUCKETS = 32
REL_MAX_DIST = 2048
N_BRANCH = 3
X_HEADS = 4
X_HEAD_DIM = D_MODEL // X_HEADS
D_FF = 2816
FFN_CONV_K = 3
NORM_EPS = 1e-6
LN_EPS = 1e-5

COL_A = 2 * CONV_CH
COL_B = 2 * SG_CH
COL_C = 3 * ATT_HEADS * HEAD_DIM
OFF_B = COL_A
OFF_C = COL_A + COL_B
OFF_G = COL_A + COL_B + COL_C
ATT_OUT = HEADS_PER_GROUP * HEAD_DIM

LANES = 128
SUBLANES = 8
VMEM_LIMIT_BYTES = 56 * 1024 * 1024

PROJ_TILE = 512
SEQ_TILE = 512
CONV_ROWS = 128
CONV_HALO = 32
FFN_HALO = 8
G2_UNROLL = 4
G0_UNROLL = 3

BF16 = jnp.bfloat16
F32 = jnp.float32
_NT = (((1,), (1,)), ((), ()))


def _rms(x, g):
    return x * lax.rsqrt(jnp.mean(x * x, axis=-1, keepdims=True) + NORM_EPS) * g


def _layer_norm(x, g, b):
    mu = jnp.mean(x, axis=-1, keepdims=True)
    xc = x - mu
    var = jnp.mean(xc * xc, axis=-1, keepdims=True)
    return xc * lax.rsqrt(var + LN_EPS) * g + b


def _sigmoid(x):
    return 1.0 / (1.0 + jnp.exp(-x))


def _gelu_tanh(x):
    c = math.sqrt(2.0 / math.pi)
    return 0.5 * x * (1.0 + jnp.tanh(c * (x + 0.044715 * (x * x * x))))


def _dot(a, b):
    return jnp.dot(a, b, preferred_element_type=F32)


def _const_spec(shape):
    n = len(shape)
    return pl.BlockSpec(shape, lambda *_: (0,) * n, pipeline_mode=pl.Buffered(1))


def _bucket_tables():
    qi = np.arange(ATT_BLOCK)[:, None]
    ki = np.arange(2 * ATT_BLOCK)[None, :]
    rel = qi + ATT_BLOCK - ki
    max_exact = REL_BUCKETS // 2
    out = []
    for window, dil in DIL_GROUPS:
        span = window // dil
        dist = np.maximum(rel * dil, 0)
        nf = np.maximum(dist, 1)
        vals = []
        for dt in (np.float32, np.float64):
            x = np.log(nf.astype(dt) / dt(max_exact)) / dt(math.log(REL_MAX_DIST / max_exact))
            vals.append(max_exact + (x * dt(REL_BUCKETS - max_exact)).astype(np.int32))
        valid = (rel >= 0) & (rel <= span)
        assert np.array_equal(vals[0][valid], vals[1][valid])
        large = np.minimum(vals[0], REL_BUCKETS - 1)
        bucket = np.where(dist < max_exact, dist, large)
        out.append(np.where(valid, bucket, -1).astype(np.int32))
    return np.stack(out)


def _bias_kernel(tab_ref, bucket_ref, out_ref):
    for gi in range(len(DIL_GROUPS)):
        bk = bucket_ref[gi]
        for hh in range(HEADS_PER_GROUP):
            h = gi * HEADS_PER_GROUP + hh
            acc = jnp.full(bk.shape, -jnp.inf, F32)
            for b in range(REL_BUCKETS):
                acc = jnp.where(bk == b, tab_ref[b, h], acc)
            out_ref[h] = acc


def _bias_tables(rel_bias):
    buckets = jnp.asarray(_bucket_tables())
    return pl.pallas_call(
        _bias_kernel,
        out_shape=jax.ShapeDtypeStruct((ATT_HEADS, ATT_BLOCK, 2 * ATT_BLOCK), F32),
        in_specs=[pl.BlockSpec(memory_space=pltpu.SMEM),
                  pl.BlockSpec(memory_space=pltpu.VMEM)],
        out_specs=pl.BlockSpec(memory_space=pltpu.VMEM),
        name="rel_bias_tables",
    )(rel_bias, buckets)


def _attend_items(items, q_s, k_s, v_s, bias_ref, m_s, l_s, acc_s, oc_ref):
    lane = lax.broadcasted_iota(jnp.int32, (ATT_BLOCK, LANES), 1)
    lo = lane < HEAD_DIM
    shape = (ATT_BLOCK, LANES)

    scores = []
    for gi, pr, q_row, k_row, nk, _, _ in items:
        cols = slice(pr * LANES, (pr + 1) * LANES)
        q2 = q_s[gi, pl.ds(q_row, ATT_BLOCK), cols]
        k2 = k_s[gi, pl.ds(k_row, nk), cols]
        ha = gi * HEADS_PER_GROUP + 2 * pr
        if nk == ATT_BLOCK:
            ba = bias_ref[ha, :, ATT_BLOCK:]
            bb = bias_ref[ha + 1, :, ATT_BLOCK:]
        else:
            ba = bias_ref[ha]
            bb = bias_ref[ha + 1]
        zero = jnp.zeros_like(q2)
        sa = lax.dot_general(jnp.where(lo, q2, zero), k2, _NT, preferred_element_type=F32) + ba
        sb = lax.dot_general(jnp.where(lo, zero, q2), k2, _NT, preferred_element_type=F32) + bb
        scores.append((sa, sb))

    stats = []
    for pair in scores:
        r = []
        for s in pair:
            m = jnp.max(s, axis=-1, keepdims=True)
            e = jnp.exp(s - m)
            r.append((m, jnp.sum(e, axis=-1, keepdims=True), e.astype(BF16)))
        stats.append(r)

    nums = []
    for (gi, pr, _, k_row, nk, _, _), r in zip(items, stats):
        v2 = v_s[gi, pl.ds(k_row, nk), pr * LANES:(pr + 1) * LANES]
        nums.append(jnp.where(lo, _dot(r[0][2], v2), _dot(r[1][2], v2)))

    for (gi, pr, _, _, _, out_rows, mode), r, a2 in zip(items, stats, nums):
        m2 = jnp.where(lo, jnp.broadcast_to(r[0][0], shape), jnp.broadcast_to(r[1][0], shape))
        l2 = jnp.where(lo, jnp.broadcast_to(r[0][1], shape), jnp.broadcast_to(r[1][1], shape))
        if mode == "init":
            m_s[pr, out_rows, :] = m2
            l_s[pr, out_rows, :] = l2
            acc_s[pr, out_rows, :] = a2
        else:
            mo = m_s[pr, out_rows, :]
            mn = jnp.maximum(mo, m2)
            eo = jnp.exp(mo - mn)
            e2 = jnp.exp(m2 - mn)
            ln = eo * l_s[pr, out_rows, :] + e2 * l2
            an = eo * acc_s[pr, out_rows, :] + e2 * a2
            if mode == "merge":
                m_s[pr, out_rows, :] = mn
                l_s[pr, out_rows, :] = ln
                acc_s[pr, out_rows, :] = an
            else:
                oc_ref[out_rows, pr * LANES:(pr + 1) * LANES] = (an / ln).astype(BF16)


def _attn_kernel(x_ref, g_ref, w_ref, bias_ref, oc_ref,
                 q_s, k_s, v_s, stage, m_s, l_s, acc_s):
    g = g_ref[...]
    hw = ATT_OUT

    def proj_tile(tt, carry):
        r0 = pl.multiple_of(tt * PROJ_TILE, PROJ_TILE)
        h = _rms(x_ref[pl.ds(r0, PROJ_TILE), :], g).astype(BF16)
        for gi, (_, dil) in enumerate(DIL_GROUPS):
            z = _dot(h, w_ref[gi])
            if dil == 1:
                q_s[gi, pl.ds(r0, PROJ_TILE), :] = (z[:, :hw] * HEAD_DIM ** -0.5).astype(BF16)
                k_s[gi, pl.ds(r0, PROJ_TILE), :] = z[:, hw:2 * hw].astype(BF16)
                v_s[gi, pl.ds(r0, PROJ_TILE), :] = z[:, 2 * hw:].astype(BF16)
            else:
                nt = hw // LANES
                for j in range(3 * nt):
                    stage[j] = z[:, j * LANES:(j + 1) * LANES]
                n = PROJ_TILE // dil
                cls_len = SEQ // dil
                for c in range(dil):
                    dst = pl.multiple_of(c * cls_len + tt * n, n)
                    for j in range(3 * nt):
                        zz = stage[j, pl.ds(c, n, stride=dil), :]
                        cols = slice((j % nt) * LANES, (j % nt + 1) * LANES)
                        if j < nt:
                            q_s[gi, pl.ds(dst, n), cols] = (zz * HEAD_DIM ** -0.5).astype(BF16)
                        elif j < 2 * nt:
                            k_s[gi, pl.ds(dst, n), cols] = zz.astype(BF16)
                        else:
                            v_s[gi, pl.ds(dst, n), cols] = zz.astype(BF16)
        return carry

    lax.fori_loop(0, SEQ // PROJ_TILE, proj_tile, 0)

    refs = (q_s, k_s, v_s, bias_ref, m_s, l_s, acc_s, oc_ref)
    pairs = range(HEADS_PER_GROUP // 2)

    dil2 = DIL_GROUPS[2][1]

    def g2_body(i, carry):
        items = []
        for u in range(G2_UNROLL):
            c = i * G2_UNROLL + u
            row = pl.multiple_of(c * ATT_BLOCK, ATT_BLOCK)
            out_rows = pl.ds(c, ATT_BLOCK, stride=dil2)
            items += [(2, pr, row, row, ATT_BLOCK, out_rows, "init") for pr in pairs]
        _attend_items(items, *refs)
        return carry

    lax.fori_loop(0, dil2 // G2_UNROLL, g2_body, 0)

    dil1 = DIL_GROUPS[1][1]
    cls1 = SEQ // dil1
    for n in range(cls1 // ATT_BLOCK):
        items = []
        for c in range(dil1):
            row = c * cls1 + n * ATT_BLOCK
            out_rows = pl.ds(n * (ATT_BLOCK * dil1) + c, ATT_BLOCK, stride=dil1)
            if n == 0:
                items += [(1, pr, row, row, ATT_BLOCK, out_rows, "merge") for pr in pairs]
            else:
                items += [(1, pr, row, row - ATT_BLOCK, 2 * ATT_BLOCK, out_rows, "merge")
                          for pr in pairs]
        _attend_items(items, *refs)

    _attend_items([(0, pr, 0, 0, ATT_BLOCK, pl.ds(0, ATT_BLOCK), "final") for pr in pairs], *refs)

    def g0_body(i, carry):
        items = []
        for u in range(G0_UNROLL):
            row = pl.multiple_of((1 + i * G0_UNROLL + u) * ATT_BLOCK, ATT_BLOCK)
            prev = pl.multiple_of(row - ATT_BLOCK, ATT_BLOCK)
            items += [(0, pr, row, prev, 2 * ATT_BLOCK, pl.ds(row, ATT_BLOCK), "final")
                      for pr in pairs]
        _attend_items(items, *refs)
        return carry

    lax.fori_loop(0, (SEQ // ATT_BLOCK - 1) // G0_UNROLL, g0_body, 0)


def _attention(x, pre_g, w_qkv, bias):
    B = x.shape[0]
    ng = len(DIL_GROUPS)
    return pl.pallas_call(
        _attn_kernel,
        out_shape=jax.ShapeDtypeStruct((B, SEQ, ATT_OUT), BF16),
        grid=(B,),
        in_specs=[
            pl.BlockSpec((None, SEQ, D_MODEL), lambda b: (b, 0, 0)),
            _const_spec((1, D_MODEL)),
            _const_spec((ng, D_MODEL, 3 * ATT_OUT)),
            _const_spec((ATT_HEADS, ATT_BLOCK, 2 * ATT_BLOCK)),
        ],
        out_specs=pl.BlockSpec((None, SEQ, ATT_OUT), lambda b: (b, 0, 0)),
        scratch_shapes=[
            pltpu.VMEM((ng, SEQ, ATT_OUT), BF16),
            pltpu.VMEM((ng, SEQ, ATT_OUT), BF16),
            pltpu.VMEM((ng, SEQ, ATT_OUT), BF16),
            pltpu.VMEM((3 * ATT_OUT // LANES, PROJ_TILE, LANES), F32),
            pltpu.VMEM((ATT_OUT // LANES, SEQ, LANES), F32),
            pltpu.VMEM((ATT_OUT // LANES, SEQ, LANES), F32),
            pltpu.VMEM((ATT_OUT // LANES, SEQ, LANES), F32),
        ],
        compiler_params=pltpu.CompilerParams(
            dimension_semantics=("arbitrary",), vmem_limit_bytes=VMEM_LIMIT_BYTES),
        name="dilated_attention",
    )(x, pre_g, w_qkv, bias)


def _mixer_kernel(x_ref, oc_ref, pre_g, post_g, w_a, w_b, w_g, b_gate,
                  conv_w, conv_b, ln_a_g, ln_a_b, w_a_out,
                  ln_b_g, ln_b_b, ws_ref, bs_ref, w_b_out, w_c_out, w_mix,
                  out_ref, abuf, a2buf):
    ts = SEQ_TILE
    s = pl.program_id(1)
    x = x_ref[...]
    h = _rms(x, pre_g[...]).astype(BF16)

    za = _dot(h, w_a[...])
    a = za[:, :CONV_CH] * _sigmoid(za[:, CONV_CH:])

    @pl.when(s == 0)
    def _():
        abuf[0:CONV_HALO, :] = jnp.zeros((CONV_HALO, CONV_CH), F32)

    @pl.when(s > 0)
    def _():
        abuf[0:CONV_HALO, :] = abuf[ts:ts + CONV_HALO, :]

    abuf[CONV_HALO:CONV_HALO + ts, :] = a
    lag = CONV_HALO - (CONV_K - 1)
    for rc in range(ts // CONV_ROWS):
        tiles = []
        for lt in range(CONV_CH // LANES):
            cs = slice(lt * LANES, (lt + 1) * LANES)
            acc = jnp.broadcast_to(conv_b[:, cs], (CONV_ROWS, LANES))
            for r in range(SUBLANES):
                rows = CONV_ROWS + (SUBLANES if r else 0)
                part = None
                for j in range(CONV_K):
                    if (lag + j) % SUBLANES != r:
                        continue
                    off = rc * CONV_ROWS + lag + j - r
                    term = conv_w[j:j + 1, cs] * abuf[off:off + rows, cs]
                    part = term if part is None else part + term
                acc = acc + part[r:r + CONV_ROWS]
            tiles.append(acc)
        y = _layer_norm(jnp.concatenate(tiles, axis=1), ln_a_g[...], ln_a_b[...])
        a2buf[rc * CONV_ROWS:(rc + 1) * CONV_ROWS, :] = (y * _sigmoid(y)).astype(BF16)
    y_a = _dot(a2buf[...], w_a_out[...])
    merged = _sigmoid(_dot(h, w_g[:, 0:D_MODEL]) + b_gate[0:1, :]) * y_a

    zb = _gelu_tanh(_dot(h, w_b[...]))
    u = zb[:, :SG_CH]
    v = _layer_norm(zb[:, SG_CH:], ln_b_g[...], ln_b_b[...]).astype(BF16)
    nch = ts // SG_CHUNK
    gw = SG_CH // SG_GROUPS
    ti = lax.broadcasted_iota(jnp.int32, (SG_CHUNK, SG_CHUNK), 0)
    si = lax.broadcasted_iota(jnp.int32, (SG_CHUNK, SG_CHUNK), 1)
    causal = si <= ti
    mixed = []
    for gi in range(SG_GROUPS):
        wsm = jnp.where(causal, ws_ref[gi], 0.0).astype(BF16)
        rhs = jnp.concatenate(
            [v[n * SG_CHUNK:(n + 1) * SG_CHUNK, gi * gw:(gi + 1) * gw] for n in range(nch)], axis=1)
        r = _dot(wsm, rhs)
        mixed.append(jnp.concatenate(
            [r[:, n * gw:(n + 1) * gw] for n in range(nch)], axis=0))
    bs_tile = jnp.concatenate([bs_ref[...]] * nch, axis=0)
    vb = jnp.concatenate(mixed, axis=1) + bs_tile
    y_b = _dot((u * vb).astype(BF16), w_b_out[...])
    merged = merged + _sigmoid(_dot(h, w_g[:, D_MODEL:2 * D_MODEL]) + b_gate[1:2, :]) * y_b

    y_c = _dot(oc_ref[...], w_c_out[...])
    merged = merged + _sigmoid(_dot(h, w_g[:, 2 * D_MODEL:]) + b_gate[2:3, :]) * y_c

    y = _dot(merged.astype(BF16), w_mix[...])
    out_ref[...] = x + _rms(y, post_g[...])


def _mixer(x, oc, pre_g, post_g, w_a, w_b, w_g, b_gate, conv_w, conv_b, ln_a_g, ln_a_b, w_a_out,
           ln_b_g, ln_b_b, w_s, bs_cols, w_b_out, w_c_out, w_mix):
    B = x.shape[0]
    ts = SEQ_TILE
    consts = (pre_g, post_g, w_a, w_b, w_g, b_gate, conv_w, conv_b, ln_a_g, ln_a_b, w_a_out,
              ln_b_g, ln_b_b, w_s, bs_cols, w_b_out, w_c_out, w_mix)
    return pl.pallas_call(
        _mixer_kernel,
        out_shape=jax.ShapeDtypeStruct(x.shape, F32),
        grid=(B, SEQ // ts),
        in_specs=[
            pl.BlockSpec((None, ts, D_MODEL), lambda b, s: (b, s, 0)),
            pl.BlockSpec((None, ts, ATT_OUT), lambda b, s: (b, s, 0)),
        ] + [_const_spec(c.shape) for c in consts],
        out_specs=pl.BlockSpec((None, ts, D_MODEL), lambda b, s: (b, s, 0)),
        scratch_shapes=[
            pltpu.VMEM((CONV_HALO + ts, CONV_CH), F32),
            pltpu.VMEM((ts, CONV_CH), BF16),
        ],
        compiler_params=pltpu.CompilerParams(
            dimension_semantics=("arbitrary", "arbitrary"), vmem_limit_bytes=VMEM_LIMIT_BYTES),
        name="parallel_mixer",
    )(x, oc, *consts)


def _xattn_kernel(x_ref, mem_ref, pre_g, post_g, mem_g, w_q, w_kv, w_o, out_ref, k_s, v_s):
    s = pl.program_id(1)

    @pl.when(s == 0)
    def _():
        mn = _rms(mem_ref[...], mem_g[...]).astype(BF16)
        kv = _dot(mn, w_kv[...])
        k_s[...] = kv[:, :D_MODEL].astype(BF16)
        v_s[...] = kv[:, D_MODEL:].astype(BF16)

    x = x_ref[...]
    h = _rms(x, pre_g[...]).astype(BF16)
    q = (_dot(h, w_q[...]) * X_HEAD_DIM ** -0.5).astype(BF16)
    outs = []
    for hd in range(X_HEADS):
        cols = slice(hd * X_HEAD_DIM, (hd + 1) * X_HEAD_DIM)
        sc = lax.dot_general(q[:, cols], k_s[:, cols], _NT, preferred_element_type=F32)
        m = jnp.max(sc, axis=-1, keepdims=True)
        e = jnp.exp(sc - m)
        p = e / jnp.sum(e, axis=-1, keepdims=True)
        outs.append(_dot(p.astype(BF16), v_s[:, cols]))
    o = jnp.concatenate(outs, axis=1).astype(BF16)
    y = _dot(o, w_o[...])
    out_ref[...] = x + _rms(y, post_g[...])


def _cross_attention(x, mem, pre_g, post_g, mem_g, w_q, w_kv, w_o):
    B = x.shape[0]
    ts = SEQ_TILE
    consts = (pre_g, post_g, mem_g, w_q, w_kv, w_o)
    return pl.pallas_call(
        _xattn_kernel,
        out_shape=jax.ShapeDtypeStruct(x.shape, F32),
        grid=(B, SEQ // ts),
        in_specs=[
            pl.BlockSpec((None, ts, D_MODEL), lambda b, s: (b, s, 0)),
            pl.BlockSpec((None, MEM_LEN, D_MODEL), lambda b, s: (b, 0, 0)),
        ] + [_const_spec(c.shape) for c in consts],
        out_specs=pl.BlockSpec((None, ts, D_MODEL), lambda b, s: (b, s, 0)),
        scratch_shapes=[
            pltpu.VMEM((MEM_LEN, D_MODEL), BF16),
            pltpu.VMEM((MEM_LEN, D_MODEL), BF16),
        ],
        compiler_params=pltpu.CompilerParams(
            dimension_semantics=("arbitrary", "arbitrary"), vmem_limit_bytes=VMEM_LIMIT_BYTES),
        name="memory_cross_attention",
    )(x, mem, *consts)


def _ffn_kernel(x_ref, pre_g, post_g, w_gate, w_val, conv_w, conv_b, w_down, out_ref, gbuf):
    ts = SEQ_TILE
    s = pl.program_id(1)
    x = x_ref[...]
    h = _rms(x, pre_g[...]).astype(BF16)

    @pl.when(s == 0)
    def _():
        gbuf[0:FFN_HALO, :] = jnp.zeros((FFN_HALO, D_FF), F32)

    @pl.when(s > 0)
    def _():
        gbuf[0:FFN_HALO, :] = gbuf[ts:ts + FFN_HALO, :]

    gbuf[FFN_HALO:FFN_HALO + ts, :] = _dot(h, w_gate[...])
    lag = FFN_HALO - (FFN_CONV_K - 1)
    gc = jnp.broadcast_to(conv_b[...], (ts, D_FF))
    for j in range(FFN_CONV_K):
        gc = gc + conv_w[j:j + 1, :] * gbuf[lag + j:lag + j + ts, :]
    act = _gelu_tanh(gc) * _dot(h, w_val[...])
    y = _dot(act.astype(BF16), w_down[...])
    out_ref[...] = x + _rms(y, post_g[...])


def _conv_ffn(x, pre_g, post_g, w_gate, w_val, conv_w, conv_b, w_down):
    B = x.shape[0]
    ts = SEQ_TILE
    consts = (pre_g, post_g, w_gate, w_val, conv_w, conv_b, w_down)
    return pl.pallas_call(
        _ffn_kernel,
        out_shape=jax.ShapeDtypeStruct(x.shape, F32),
        grid=(B, SEQ // ts),
        in_specs=[pl.BlockSpec((None, ts, D_MODEL), lambda b, s: (b, s, 0))]
        + [_const_spec(c.shape) for c in consts],
        out_specs=pl.BlockSpec((None, ts, D_MODEL), lambda b, s: (b, s, 0)),
        scratch_shapes=[pltpu.VMEM((FFN_HALO + ts, D_FF), F32)],
        compiler_params=pltpu.CompilerParams(
            dimension_semantics=("arbitrary", "arbitrary"), vmem_limit_bytes=VMEM_LIMIT_BYTES),
        name="conv_ffn",
    )(x, *consts)


def _row(v):
    return v.reshape(1, -1)


def _qkv_weights(w_c):
    n = ATT_HEADS * HEAD_DIM
    parts = []
    for gi in range(len(DIL_GROUPS)):
        cols = [w_c[:, t * n + gi * ATT_OUT:t * n + (gi + 1) * ATT_OUT] for t in range(3)]
        parts.append(jnp.concatenate(cols, axis=1))
    return jnp.stack(parts).astype(BF16)


def kernel(x, mem, rel_bias, mix_pre_g, mix_post_g, w_in, b_gate, conv_a_w, conv_a_b, ln_a_g, ln_a_b, w_a_out, ln_b_g, ln_b_b, w_s, b_s, w_b_out, w_c_out, w_mix_out, x_pre_g, x_post_g, mem_g, w_xq, w_xkv, w_xo, ffn_pre_g, ffn_post_g, w_up, conv_f_w, conv_f_b, w_down):
    assert x.shape[1:] == (SEQ, D_MODEL) and mem.shape[1:] == (MEM_LEN, D_MODEL)
    depth = w_in.shape[0]
    bias = _bias_tables(rel_bias)
    gw = SG_CH // SG_GROUPS
    for l in range(depth):
        w = w_in[l]
        oc = _attention(x, _row(mix_pre_g[l]), _qkv_weights(w[:, OFF_C:OFF_G]), bias)
        bs_cols = jnp.repeat(b_s[l].T, gw, axis=1)
        x = _mixer(
            x, oc, _row(mix_pre_g[l]), _row(mix_post_g[l]),
            w[:, :OFF_B].astype(BF16), w[:, OFF_B:OFF_C].astype(BF16), w[:, OFF_G:].astype(BF16),
            b_gate[l], conv_a_w[l], _row(conv_a_b[l]), _row(ln_a_g[l]), _row(ln_a_b[l]),
            w_a_out[l].astype(BF16), _row(ln_b_g[l]), _row(ln_b_b[l]), w_s[l], bs_cols,
            w_b_out[l].astype(BF16), w_c_out[l].astype(BF16), w_mix_out[l].astype(BF16))
        x = _cross_attention(
            x, mem, _row(x_pre_g[l]), _row(x_post_g[l]), _row(mem_g[l]),
            w_xq[l].astype(BF16), w_xkv[l].astype(BF16), w_xo[l].astype(BF16))
        x = _conv_ffn(
            x, _row(ffn_pre_g[l]), _row(ffn_post_g[l]),
            w_up[l][:, :D_FF].astype(BF16), w_up[l][:, D_FF:].astype(BF16),
            conv_f_w[l], _row(conv_f_b[l]), w_down[l].astype(BF16))
    return x
```

```python
import functools
import math

import numpy as np
import jax
import jax.numpy as jnp
from jax import lax
from jax.experimental import pallas as pl
from jax.experimental.pallas import tpu as pltpu

D_MODEL = 1024
SEQ = 2048
MEM_LEN = 256
CONV_CH = 512
CONV_K = 31
SG_CH = 512
SG_GROUPS = 4
SG_CHUNK = 128
HEAD_DIM = 64
HEADS_PER_GROUP = 4
DIL_GROUPS = ((128, 1), (512, 4), (2048, 16))
ATT_HEADS = HEADS_PER_GROUP * len(DIL_GROUPS)
ATT_BLOCK = 128
REL_BUCKETS = 32
REL_MAX_DIST = 2048
N_BRANCH = 3
X_HEADS = 4
X_HEAD_DIM = D_MODEL // X_HEADS
D_FF = 2816
FFN_CONV_K = 3
NORM_EPS = 1e-6
LN_EPS = 1e-5

COL_A = 2 * CONV_CH
COL_B = 2 * SG_CH
COL_C = 3 * ATT_HEADS * HEAD_DIM
OFF_B = COL_A
OFF_C = COL_A + COL_B
OFF_G = COL_A + COL_B + COL_C
ATT_OUT = HEADS_PER_GROUP * HEAD_DIM

LANES = 128
SUBLANES = 8
VMEM_LIMIT_BYTES = 56 * 1024 * 1024

PROJ_TILE = 512
SEQ_TILE = 512
CONV_ROWS = 128
CONV_HALO = 32
FFN_HALO = 8
ZB_COLS = 256
G2_UNROLL = 4
G0_UNROLL = 3

BF16 = jnp.bfloat16
F32 = jnp.float32
_NT = (((1,), (1,)), ((), ()))


def _rms(x, g):
    return x * lax.rsqrt(jnp.mean(x * x, axis=-1, keepdims=True) + NORM_EPS) * g


def _layer_norm(x, g, b):
    mu = jnp.mean(x, axis=-1, keepdims=True)
    xc = x - mu
    var = jnp.mean(xc * xc, axis=-1, keepdims=True)
    return xc * lax.rsqrt(var + LN_EPS) * g + b


def _sigmoid(x):
    return 1.0 / (1.0 + jnp.exp(-x))


def _gelu_tanh(x):
    c = math.sqrt(2.0 / math.pi)
    return 0.5 * x * (1.0 + jnp.tanh(c * (x + 0.044715 * (x * x * x))))


def _dot(a, b):
    return jnp.dot(a, b, preferred_element_type=F32)


def _const_spec(shape):
    n = len(shape)
    return pl.BlockSpec(shape, lambda *_: (0,) * n, pipeline_mode=pl.Buffered(1))


def _bucket_tables():
    qi = np.arange(ATT_BLOCK)[:, None]
    ki = np.arange(2 * ATT_BLOCK)[None, :]
    rel = qi + ATT_BLOCK - ki
    max_exact = REL_BUCKETS // 2
    out = []
    for window, dil in DIL_GROUPS:
        span = window // dil
        dist = np.maximum(rel * dil, 0)
        nf = np.maximum(dist, 1)
        vals = []
        for dt in (np.float32, np.float64):
            x = np.log(nf.astype(dt) / dt(max_exact)) / dt(math.log(REL_MAX_DIST / max_exact))
            vals.append(max_exact + (x * dt(REL_BUCKETS - max_exact)).astype(np.int32))
        valid = (rel >= 0) & (rel <= span)
        assert np.array_equal(vals[0][valid], vals[1][valid])
        large = np.minimum(vals[0], REL_BUCKETS - 1)
        bucket = np.where(dist < max_exact, dist, large)
        out.append(np.where(valid, bucket, -1).astype(np.int32))
    return np.stack(out)


def _bias_kernel(tab_ref, bucket_ref, out_ref):
    for gi in range(len(DIL_GROUPS)):
        bk = bucket_ref[gi]
        for hh in range(HEADS_PER_GROUP):
            h = gi * HEADS_PER_GROUP + hh
            acc = jnp.full(bk.shape, -jnp.inf, F32)
            for b in range(REL_BUCKETS):
                acc = jnp.where(bk == b, tab_ref[b, h], acc)
            out_ref[h] = acc


def _bias_tables(rel_bias):
    buckets = jnp.asarray(_bucket_tables())
    return pl.pallas_call(
        _bias_kernel,
        out_shape=jax.ShapeDtypeStruct((ATT_HEADS, ATT_BLOCK, 2 * ATT_BLOCK), F32),
        in_specs=[pl.BlockSpec(memory_space=pltpu.SMEM),
                  pl.BlockSpec(memory_space=pltpu.VMEM)],
        out_specs=pl.BlockSpec(memory_space=pltpu.VMEM),
        name="rel_bias_tables",
    )(rel_bias, buckets)


def _attend_items(items, q_s, k_s, v_s, bias_ref, m_s, l_s, acc_s, oc_ref):
    lane = lax.broadcasted_iota(jnp.int32, (ATT_BLOCK, LANES), 1)
    lo = lane < HEAD_DIM
    shape = (ATT_BLOCK, LANES)

    scores = []
    for gi, pr, q_row, k_row, nk, _, _ in items:
        cols = slice(pr * LANES, (pr + 1) * LANES)
        q2 = q_s[gi, pl.ds(q_row, ATT_BLOCK), cols]
        k2 = k_s[gi, pl.ds(k_row, nk), cols]
        ha = gi * HEADS_PER_GROUP + 2 * pr
        if nk == ATT_BLOCK:
            ba = bias_ref[ha, :, ATT_BLOCK:]
            bb = bias_ref[ha + 1, :, ATT_BLOCK:]
        else:
            ba = bias_ref[ha]
            bb = bias_ref[ha + 1]
        zero = jnp.zeros_like(q2)
        sa = lax.dot_general(jnp.where(lo, q2, zero), k2, _NT, preferred_element_type=F32) + ba
        sb = lax.dot_general(jnp.where(lo, zero, q2), k2, _NT, preferred_element_type=F32) + bb
        scores.append((sa, sb))

    stats = []
    for pair in scores:
        r = []
        for s in pair:
            m = jnp.max(s, axis=-1, keepdims=True)
            e = jnp.exp(s - m)
            r.append((m, jnp.sum(e, axis=-1, keepdims=True), e.astype(BF16)))
        stats.append(r)

    nums = []
    for (gi, pr, _, k_row, nk, _, _), r in zip(items, stats):
        v2 = v_s[gi, pl.ds(k_row, nk), pr * LANES:(pr + 1) * LANES]
        nums.append(jnp.where(lo, _dot(r[0][2], v2), _dot(r[1][2], v2)))

    for (gi, pr, _, _, _, out_rows, mode), r, a2 in zip(items, stats, nums):
        m2 = jnp.where(lo, jnp.broadcast_to(r[0][0], shape), jnp.broadcast_to(r[1][0], shape))
        l2 = jnp.where(lo, jnp.broadcast_to(r[0][1], shape), jnp.broadcast_to(r[1][1], shape))
        if mode == "init":
            m_s[pr, out_rows, :] = m2
            l_s[pr, out_rows, :] = l2
            acc_s[pr, out_rows, :] = a2
        else:
            mo = m_s[pr, out_rows, :]
            mn = jnp.maximum(mo, m2)
            eo = jnp.exp(mo - mn)
            e2 = jnp.exp(m2 - mn)
            ln = eo * l_s[pr, out_rows, :] + e2 * l2
            an = eo * acc_s[pr, out_rows, :] + e2 * a2
            if mode == "merge":
                m_s[pr, out_rows, :] = mn
                l_s[pr, out_rows, :] = ln
                acc_s[pr, out_rows, :] = an
            else:
                oc_ref[out_rows, pr * LANES:(pr + 1) * LANES] = (an / ln).astype(BF16)


def _attn_kernel(x_ref, g_ref, w_ref, bias_ref, oc_ref,
                 q_s, k_s, v_s, stage, m_s, l_s, acc_s):
    g = g_ref[...]
    hw = ATT_OUT

    def proj_tile(tt, carry):
        r0 = pl.multiple_of(tt * PROJ_TILE, PROJ_TILE)
        h = _rms(x_ref[pl.ds(r0, PROJ_TILE), :], g).astype(BF16)
        for gi, (_, dil) in enumerate(DIL_GROUPS):
            z = _dot(h, w_ref[gi])
            if dil == 1:
                q_s[gi, pl.ds(r0, PROJ_TILE), :] = (z[:, :hw] * HEAD_DIM ** -0.5).astype(BF16)
                k_s[gi, pl.ds(r0, PROJ_TILE), :] = z[:, hw:2 * hw].astype(BF16)
                v_s[gi, pl.ds(r0, PROJ_TILE), :] = z[:, 2 * hw:].astype(BF16)
            else:
                nt = hw // LANES
                for j in range(3 * nt):
                    stage[j] = z[:, j * LANES:(j + 1) * LANES]
                n = PROJ_TILE // dil
                cls_len = SEQ // dil
                for c in range(dil):
                    dst = pl.multiple_of(c * cls_len + tt * n, n)
                    for j in range(3 * nt):
                        zz = stage[j, pl.ds(c, n, stride=dil), :]
                        cols = slice((j % nt) * LANES, (j % nt + 1) * LANES)
                        if j < nt:
                            q_s[gi, pl.ds(dst, n), cols] = (zz * HEAD_DIM ** -0.5).astype(BF16)
                        elif j < 2 * nt:
                            k_s[gi, pl.ds(dst, n), cols] = zz.astype(BF16)
                        else:
                            v_s[gi, pl.ds(dst, n), cols] = zz.astype(BF16)
        return carry

    lax.fori_loop(0, SEQ // PROJ_TILE, proj_tile, 0)

    refs = (q_s, k_s, v_s, bias_ref, m_s, l_s, acc_s, oc_ref)
    pairs = range(HEADS_PER_GROUP // 2)

    dil2 = DIL_GROUPS[2][1]

    def g2_body(i, carry):
        items = []
        for u in range(G2_UNROLL):
            c = i * G2_UNROLL + u
            row = pl.multiple_of(c * ATT_BLOCK, ATT_BLOCK)
            out_rows = pl.ds(c, ATT_BLOCK, stride=dil2)
            items += [(2, pr, row, row, ATT_BLOCK, out_rows, "init") for pr in pairs]
        _attend_items(items, *refs)
        return carry

    lax.fori_loop(0, dil2 // G2_UNROLL, g2_body, 0)

    dil1 = DIL_GROUPS[1][1]
    cls1 = SEQ // dil1
    for n in range(cls1 // ATT_BLOCK):
        items = []
        for c in range(dil1):
            row = c * cls1 + n * ATT_BLOCK
            out_rows = pl.ds(n * (ATT_BLOCK * dil1) + c, ATT_BLOCK, stride=dil1)
            if n == 0:
                items += [(1, pr, row, row, ATT_BLOCK, out_rows, "merge") for pr in pairs]
            else:
                items += [(1, pr, row, row - ATT_BLOCK, 2 * ATT_BLOCK, out_rows, "merge")
                          for pr in pairs]
        _attend_items(items, *refs)

    _attend_items([(0, pr, 0, 0, ATT_BLOCK, pl.ds(0, ATT_BLOCK), "final") for pr in pairs], *refs)

    def g0_body(i, carry):
        items = []
        for u in range(G0_UNROLL):
            row = pl.multiple_of((1 + i * G0_UNROLL + u) * ATT_BLOCK, ATT_BLOCK)
            prev = pl.multiple_of(row - ATT_BLOCK, ATT_BLOCK)
            items += [(0, pr, row, prev, 2 * ATT_BLOCK, pl.ds(row, ATT_BLOCK), "final")
                      for pr in pairs]
        _attend_items(items, *refs)
        return carry

    lax.fori_loop(0, (SEQ // ATT_BLOCK - 1) // G0_UNROLL, g0_body, 0)


def _attention(x, pre_g, w_qkv, bias):
    B = x.shape[0]
    ng = len(DIL_GROUPS)
    return pl.pallas_call(
        _attn_kernel,
        out_shape=jax.ShapeDtypeStruct((B, SEQ, ATT_OUT), BF16),
        grid=(B,),
        in_specs=[
            pl.BlockSpec((None, SEQ, D_MODEL), lambda b: (b, 0, 0)),
            _const_spec((1, D_MODEL)),
            _const_spec((ng, D_MODEL, 3 * ATT_OUT)),
            _const_spec((ATT_HEADS, ATT_BLOCK, 2 * ATT_BLOCK)),
        ],
        out_specs=pl.BlockSpec((None, SEQ, ATT_OUT), lambda b: (b, 0, 0)),
        scratch_shapes=[
            pltpu.VMEM((ng, SEQ, ATT_OUT), BF16),
            pltpu.VMEM((ng, SEQ, ATT_OUT), BF16),
            pltpu.VMEM((ng, SEQ, ATT_OUT), BF16),
            pltpu.VMEM((3 * ATT_OUT // LANES, PROJ_TILE, LANES), F32),
            pltpu.VMEM((ATT_OUT // LANES, SEQ, LANES), F32),
            pltpu.VMEM((ATT_OUT // LANES, SEQ, LANES), F32),
            pltpu.VMEM((ATT_OUT // LANES, SEQ, LANES), F32),
        ],
        compiler_params=pltpu.CompilerParams(
            dimension_semantics=("arbitrary",), vmem_limit_bytes=VMEM_LIMIT_BYTES),
        name="dilated_attention",
    )(x, pre_g, w_qkv, bias)


def _mixer_kernel(x_ref, oc_ref, pre_g, post_g, w_a, w_gb, b_gb,
                  conv_w, conv_b, ln_a_g, ln_a_b, w_a_out,
                  ln_b_g, ln_b_b, ws_ref, bs_ref, w_b_out, w_c_out, w_mix,
                  out_ref, hbuf, abuf, cbuf, a2buf, zbuf):
    ts = SEQ_TILE
    s = pl.program_id(1)
    x = x_ref[...]
    h = _rms(x, pre_g[...]).astype(BF16)
    hbuf[...] = h

    za = _dot(h, w_a[...])
    a = za[:, :CONV_CH] * _sigmoid(za[:, CONV_CH:])

    @pl.when(s == 0)
    def _():
        abuf[0:CONV_HALO, :] = jnp.zeros((CONV_HALO, CONV_CH), F32)

    @pl.when(s > 0)
    def _():
        abuf[0:CONV_HALO, :] = abuf[ts:ts + CONV_HALO, :]

    abuf[CONV_HALO:CONV_HALO + ts, :] = a
    lag = CONV_HALO - (CONV_K - 1)
    n_lt = CONV_CH // LANES
    n_rc = ts // CONV_ROWS
    n_zb = w_gb.shape[0]
    zb_per_step = n_zb // n_rc

    def conv_step(rc, carry):
        r0 = pl.multiple_of(rc * CONV_ROWS, CONV_ROWS)
        for lt in range(n_lt):
            cs = slice(lt * LANES, (lt + 1) * LANES)
            acc = jnp.broadcast_to(conv_b[:, cs], (CONV_ROWS, LANES))
            for r in range(SUBLANES):
                rows = CONV_ROWS + (SUBLANES if r else 0)
                part = None
                for j in range(CONV_K):
                    if (lag + j) % SUBLANES != r:
                        continue
                    off = pl.multiple_of(r0 + (lag + j - r), SUBLANES)
                    term = conv_w[j:j + 1, cs] * abuf[pl.ds(off, rows), cs]
                    part = term if part is None else part + term
                acc = acc + part[r:r + CONV_ROWS]
            cbuf[pl.ds(r0, CONV_ROWS), cs] = acc
        for k in range(zb_per_step):
            idx = rc * zb_per_step + k
            zbuf[idx] = _dot(hbuf[...], w_gb[idx]) + b_gb[idx]
        return carry

    lax.fori_loop(0, n_rc, conv_step, 0)

    def zcols(first, count):
        return jnp.concatenate([zbuf[first + i] for i in range(count)], axis=1)

    per_gate = D_MODEL // ZB_COLS
    for rc in range(ts // CONV_ROWS):
        rs = slice(rc * CONV_ROWS, (rc + 1) * CONV_ROWS)
        y = _layer_norm(cbuf[rs, :], ln_a_g[...], ln_a_b[...])
        a2buf[rs, :] = (y * _sigmoid(y)).astype(BF16)
    y_a = _dot(a2buf[...], w_a_out[...])
    merged = _sigmoid(zcols(0, per_gate)) * y_a

    zb = _gelu_tanh(zcols(N_BRANCH * per_gate, COL_B // ZB_COLS))
    u = zb[:, :SG_CH]
    v = _layer_norm(zb[:, SG_CH:], ln_b_g[...], ln_b_b[...]).astype(BF16)
    nch = ts // SG_CHUNK
    gw = SG_CH // SG_GROUPS
    ti = lax.broadcasted_iota(jnp.int32, (SG_CHUNK, SG_CHUNK), 0)
    si = lax.broadcasted_iota(jnp.int32, (SG_CHUNK, SG_CHUNK), 1)
    causal = si <= ti
    mixed = []
    for gi in range(SG_GROUPS):
        wsm = jnp.where(causal, ws_ref[gi], 0.0).astype(BF16)
        rhs = jnp.concatenate(
            [v[n * SG_CHUNK:(n + 1) * SG_CHUNK, gi * gw:(gi + 1) * gw] for n in range(nch)], axis=1)
        r = _dot(wsm, rhs)
        mixed.append(jnp.concatenate(
            [r[:, n * gw:(n + 1) * gw] for n in range(nch)], axis=0))
    bs_tile = jnp.concatenate([bs_ref[...]] * nch, axis=0)
    vb = jnp.concatenate(mixed, axis=1) + bs_tile
    y_b = _dot((u * vb).astype(BF16), w_b_out[...])
    merged = merged + _sigmoid(zcols(per_gate, per_gate)) * y_b

    y_c = _dot(oc_ref[...], w_c_out[...])
    merged = merged + _sigmoid(zcols(2 * per_gate, per_gate)) * y_c

    y = _dot(merged.astype(BF16), w_mix[...])
    out_ref[...] = x + _rms(y, post_g[...])


def _mixer(x, oc, pre_g, post_g, w_a, w_gb, b_gb, conv_w, conv_b, ln_a_g, ln_a_b, w_a_out,
           ln_b_g, ln_b_b, w_s, bs_cols, w_b_out, w_c_out, w_mix):
    B = x.shape[0]
    ts = SEQ_TILE
    consts = (pre_g, post_g, w_a, w_gb, b_gb, conv_w, conv_b, ln_a_g, ln_a_b, w_a_out,
              ln_b_g, ln_b_b, w_s, bs_cols, w_b_out, w_c_out, w_mix)
    return pl.pallas_call(
        _mixer_kernel,
        out_shape=jax.ShapeDtypeStruct(x.shape, F32),
        grid=(B, SEQ // ts),
        in_specs=[
            pl.BlockSpec((None, ts, D_MODEL), lambda b, s: (b, s, 0)),
            pl.BlockSpec((None, ts, ATT_OUT), lambda b, s: (b, s, 0)),
        ] + [_const_spec(c.shape) for c in consts],
        out_specs=pl.BlockSpec((None, ts, D_MODEL), lambda b, s: (b, s, 0)),
        scratch_shapes=[
            pltpu.VMEM((ts, D_MODEL), BF16),
            pltpu.VMEM((CONV_HALO + ts, CONV_CH), F32),
            pltpu.VMEM((ts, CONV_CH), F32),
            pltpu.VMEM((ts, CONV_CH), BF16),
            pltpu.VMEM(((N_BRANCH * D_MODEL + COL_B) // ZB_COLS, ts, ZB_COLS), F32),
        ],
        compiler_params=pltpu.CompilerParams(
            dimension_semantics=("arbitrary", "arbitrary"), vmem_limit_bytes=VMEM_LIMIT_BYTES),
        name="parallel_mixer",
    )(x, oc, *consts)


def _xattn_kernel(x_ref, mem_ref, pre_g, post_g, mem_g, w_q, w_kv, w_o, out_ref, k_s, v_s):
    s = pl.program_id(1)

    @pl.when(s == 0)
    def _():
        mn = _rms(mem_ref[...], mem_g[...]).astype(BF16)
        kv = _dot(mn, w_kv[...])
        k_s[...] = kv[:, :D_MODEL].astype(BF16)
        v_s[...] = kv[:, D_MODEL:].astype(BF16)

    x = x_ref[...]
    h = _rms(x, pre_g[...]).astype(BF16)
    q = (_dot(h, w_q[...]) * X_HEAD_DIM ** -0.5).astype(BF16)
    heads = [slice(hd * X_HEAD_DIM, (hd + 1) * X_HEAD_DIM) for hd in range(X_HEADS)]
    scores = [lax.dot_general(q[:, c], k_s[:, c], _NT, preferred_element_type=F32) for c in heads]
    probs = []
    for sc in scores:
        e = jnp.exp(sc - jnp.max(sc, axis=-1, keepdims=True))
        probs.append((e / jnp.sum(e, axis=-1, keepdims=True)).astype(BF16))
    outs = [_dot(p, v_s[:, c]) for p, c in zip(probs, heads)]
    o = jnp.concatenate(outs, axis=1).astype(BF16)
    y = _dot(o, w_o[...])
    out_ref[...] = x + _rms(y, post_g[...])


def _cross_attention(x, mem, pre_g, post_g, mem_g, w_q, w_kv, w_o):
    B = x.shape[0]
    ts = SEQ_TILE
    consts = (pre_g, post_g, mem_g, w_q, w_kv, w_o)
    return pl.pallas_call(
        _xattn_kernel,
        out_shape=jax.ShapeDtypeStruct(x.shape, F32),
        grid=(B, SEQ // ts),
        in_specs=[
            pl.BlockSpec((None, ts, D_MODEL), lambda b, s: (b, s, 0)),
            pl.BlockSpec((None, MEM_LEN, D_MODEL), lambda b, s: (b, 0, 0)),
        ] + [_const_spec(c.shape) for c in consts],
        out_specs=pl.BlockSpec((None, ts, D_MODEL), lambda b, s: (b, s, 0)),
        scratch_shapes=[
            pltpu.VMEM((MEM_LEN, D_MODEL), BF16),
            pltpu.VMEM((MEM_LEN, D_MODEL), BF16),
        ],
        compiler_params=pltpu.CompilerParams(
            dimension_semantics=("arbitrary", "arbitrary"), vmem_limit_bytes=VMEM_LIMIT_BYTES),
        name="memory_cross_attention",
    )(x, mem, *consts)


def _ffn_kernel(x_ref, pre_g, post_g, w_gate, w_val, conv_w, conv_b, w_down, out_ref, gbuf):
    ts = SEQ_TILE
    s = pl.program_id(1)
    x = x_ref[...]
    h = _rms(x, pre_g[...]).astype(BF16)

    @pl.when(s == 0)
    def _():
        gbuf[0:FFN_HALO, :] = jnp.zeros((FFN_HALO, D_FF), F32)

    @pl.when(s > 0)
    def _():
        gbuf[0:FFN_HALO, :] = gbuf[ts:ts + FFN_HALO, :]

    gbuf[FFN_HALO:FFN_HALO + ts, :] = _dot(h, w_gate[...])
    lag = FFN_HALO - (FFN_CONV_K - 1)
    gc = jnp.broadcast_to(conv_b[...], (ts, D_FF))
    for j in range(FFN_CONV_K):
        gc = gc + conv_w[j:j + 1, :] * gbuf[lag + j:lag + j + ts, :]
    act = _gelu_tanh(gc) * _dot(h, w_val[...])
    y = _dot(act.astype(BF16), w_down[...])
    out_ref[...] = x + _rms(y, post_g[...])


def _conv_ffn(x, pre_g, post_g, w_gate, w_val, conv_w, conv_b, w_down):
    B = x.shape[0]
    ts = SEQ_TILE
    consts = (pre_g, post_g, w_gate, w_val, conv_w, conv_b, w_down)
    return pl.pallas_call(
        _ffn_kernel,
        out_shape=jax.ShapeDtypeStruct(x.shape, F32),
        grid=(B, SEQ // ts),
        in_specs=[pl.BlockSpec((None, ts, D_MODEL), lambda b, s: (b, s, 0))]
        + [_const_spec(c.shape) for c in consts],
        out_specs=pl.BlockSpec((None, ts, D_MODEL), lambda b, s: (b, s, 0)),
        scratch_shapes=[pltpu.VMEM((FFN_HALO + ts, D_FF), F32)],
        compiler_params=pltpu.CompilerParams(
            dimension_semantics=("arbitrary", "arbitrary"), vmem_limit_bytes=VMEM_LIMIT_BYTES),
        name="conv_ffn",
    )(x, *consts)


def _row(v):
    return v.reshape(1, -1)


def _column_blocks(w_gate, w_b):
    w = jnp.concatenate([w_gate, w_b], axis=1).astype(BF16)
    return w.reshape(D_MODEL, -1, ZB_COLS).transpose(1, 0, 2)


def _bias_blocks(b_gate):
    b = jnp.concatenate([b_gate.reshape(-1), jnp.zeros((COL_B,), F32)])
    return b.reshape(-1, 1, ZB_COLS)


def _qkv_weights(w_c):
    n = ATT_HEADS * HEAD_DIM
    parts = []
    for gi in range(len(DIL_GROUPS)):
        cols = [w_c[:, t * n + gi * ATT_OUT:t * n + (gi + 1) * ATT_OUT] for t in range(3)]
        parts.append(jnp.concatenate(cols, axis=1))
    return jnp.stack(parts).astype(BF16)


def kernel(x, mem, rel_bias, mix_pre_g, mix_post_g, w_in, b_gate, conv_a_w, conv_a_b, ln_a_g, ln_a_b, w_a_out, ln_b_g, ln_b_b, w_s, b_s, w_b_out, w_c_out, w_mix_out, x_pre_g, x_post_g, mem_g, w_xq, w_xkv, w_xo, ffn_pre_g, ffn_post_g, w_up, conv_f_w, conv_f_b, w_down):
    assert x.shape[1:] == (SEQ, D_MODEL) and mem.shape[1:] == (MEM_LEN, D_MODEL)
    depth = w_in.shape[0]
    bias = _bias_tables(rel_bias)
    gw = SG_CH // SG_GROUPS
    for l in range(depth):
        w = w_in[l]
        oc = _attention(x, _row(mix_pre_g[l]), _qkv_weights(w[:, OFF_C:OFF_G]), bias)
        bs_cols = jnp.repeat(b_s[l].T, gw, axis=1)
        x = _mixer(
            x, oc, _row(mix_pre_g[l]), _row(mix_post_g[l]),
            w[:, :OFF_B].astype(BF16), _column_blocks(w[:, OFF_G:], w[:, OFF_B:OFF_C]),
            _bias_blocks(b_gate[l]), conv_a_w[l], _row(conv_a_b[l]), _row(ln_a_g[l]), _row(ln_a_b[l]),
            w_a_out[l].astype(BF16), _row(ln_b_g[l]), _row(ln_b_b[l]), w_s[l], bs_cols,
            w_b_out[l].astype(BF16), w_c_out[l].astype(BF16), w_mix_out[l].astype(BF16))
        x = _cross_attention(
            x, mem, _row(x_pre_g[l]), _row(x_post_g[l]), _row(mem_g[l]),
            w_xq[l].astype(BF16), w_xkv[l].astype(BF16), w_xo[l].astype(BF16))
        x = _conv_ffn(
            x, _row(ffn_pre_g[l]), _row(ffn_post_g[l]),
            w_up[l][:, :D_FF].astype(BF16), w_up[l][:, D_FF:].astype(BF16),
            conv_f_w[l], _row(conv_f_b[l]), w_down[l].astype(BF16))
    return x
```

```python
import functools
import math

import numpy as np
import jax
import jax.numpy as jnp
from jax import lax
from jax.experimental import pallas as pl
from jax.experimental.pallas import tpu as pltpu

D_MODEL = 1024
SEQ = 2048
MEM_LEN = 256
CONV_CH = 512
CONV_K = 31
SG_CH = 512
SG_GROUPS = 4
SG_CHUNK = 128
HEAD_DIM = 64
HEADS_PER_GROUP = 4
DIL_GROUPS = ((128, 1), (512, 4), (2048, 16))
ATT_HEADS = HEADS_PER_GROUP * len(DIL_GROUPS)
ATT_BLOCK = 128
REL_BUCKETS = 32
REL_MAX_DIST = 2048
N_BRANCH = 3
X_HEADS = 4
X_HEAD_DIM = D_MODEL // X_HEADS
D_FF = 2816
FFN_CONV_K = 3
NORM_EPS = 1e-6
LN_EPS = 1e-5

COL_A = 2 * CONV_CH
COL_B = 2 * SG_CH
COL_C = 3 * ATT_HEADS * HEAD_DIM
OFF_B = COL_A
OFF_C = COL_A + COL_B
OFF_G = COL_A + COL_B + COL_C
ATT_OUT = HEADS_PER_GROUP * HEAD_DIM

LANES = 128
SUBLANES = 8
VMEM_LIMIT_BYTES = 56 * 1024 * 1024

PROJ_TILE = 512
SEQ_TILE = 512
CONV_ROWS = 128
CONV_HALO = 32
FFN_HALO = 8
ZB_COLS = 256
G2_UNROLL = 4
G0_UNROLL = 3

BF16 = jnp.bfloat16
F32 = jnp.float32
_NT = (((1,), (1,)), ((), ()))


def _rms(x, g):
    return x * lax.rsqrt(jnp.mean(x * x, axis=-1, keepdims=True) + NORM_EPS) * g


def _layer_norm(x, g, b):
    mu = jnp.mean(x, axis=-1, keepdims=True)
    xc = x - mu
    var = jnp.mean(xc * xc, axis=-1, keepdims=True)
    return xc * lax.rsqrt(var + LN_EPS) * g + b


def _sigmoid(x):
    return 1.0 / (1.0 + jnp.exp(-x))


def _gelu_tanh(x):
    c = math.sqrt(2.0 / math.pi)
    return 0.5 * x * (1.0 + jnp.tanh(c * (x + 0.044715 * (x * x * x))))


def _dot(a, b):
    return jnp.dot(a, b, preferred_element_type=F32)


def _const_spec(shape):
    n = len(shape)
    return pl.BlockSpec(shape, lambda *_: (0,) * n, pipeline_mode=pl.Buffered(1))


def _bucket_tables():
    qi = np.arange(ATT_BLOCK)[:, None]
    ki = np.arange(2 * ATT_BLOCK)[None, :]
    rel = qi + ATT_BLOCK - ki
    max_exact = REL_BUCKETS // 2
    out = []
    for window, dil in DIL_GROUPS:
        span = window // dil
        dist = np.maximum(rel * dil, 0)
        nf = np.maximum(dist, 1)
        vals = []
        for dt in (np.float32, np.float64):
            x = np.log(nf.astype(dt) / dt(max_exact)) / dt(math.log(REL_MAX_DIST / max_exact))
            vals.append(max_exact + (x * dt(REL_BUCKETS - max_exact)).astype(np.int32))
        valid = (rel >= 0) & (rel <= span)
        assert np.array_equal(vals[0][valid], vals[1][valid])
        large = np.minimum(vals[0], REL_BUCKETS - 1)
        bucket = np.where(dist < max_exact, dist, large)
        out.append(np.where(valid, bucket, -1).astype(np.int32))
    return np.stack(out)


def _bias_kernel(tab_ref, bucket_ref, out_ref):
    for gi in range(len(DIL_GROUPS)):
        bk = bucket_ref[gi]
        for hh in range(HEADS_PER_GROUP):
            h = gi * HEADS_PER_GROUP + hh
            acc = jnp.full(bk.shape, -jnp.inf, F32)
            for b in range(REL_BUCKETS):
                acc = jnp.where(bk == b, tab_ref[b, h], acc)
            out_ref[h] = acc


def _bias_tables(rel_bias):
    buckets = jnp.asarray(_bucket_tables())
    return pl.pallas_call(
        _bias_kernel,
        out_shape=jax.ShapeDtypeStruct((ATT_HEADS, ATT_BLOCK, 2 * ATT_BLOCK), F32),
        in_specs=[pl.BlockSpec(memory_space=pltpu.SMEM),
                  pl.BlockSpec(memory_space=pltpu.VMEM)],
        out_specs=pl.BlockSpec(memory_space=pltpu.VMEM),
        name="rel_bias_tables",
    )(rel_bias, buckets)


def _attend_items(items, q_s, k_s, v_s, bias_ref, m_s, l_s, acc_s, oc_ref):
    lane = lax.broadcasted_iota(jnp.int32, (ATT_BLOCK, LANES), 1)
    lo = lane < HEAD_DIM
    shape = (ATT_BLOCK, LANES)

    scores = []
    for gi, pr, q_row, k_row, nk, _, _ in items:
        cols = slice(pr * LANES, (pr + 1) * LANES)
        q2 = q_s[gi, pl.ds(q_row, ATT_BLOCK), cols]
        k2 = k_s[gi, pl.ds(k_row, nk), cols]
        ha = gi * HEADS_PER_GROUP + 2 * pr
        if nk == ATT_BLOCK:
            ba = bias_ref[ha, :, ATT_BLOCK:]
            bb = bias_ref[ha + 1, :, ATT_BLOCK:]
        else:
            ba = bias_ref[ha]
            bb = bias_ref[ha + 1]
        zero = jnp.zeros_like(q2)
        sa = lax.dot_general(jnp.where(lo, q2, zero), k2, _NT, preferred_element_type=F32) + ba
        sb = lax.dot_general(jnp.where(lo, zero, q2), k2, _NT, preferred_element_type=F32) + bb
        scores.append((sa, sb))

    stats = []
    for pair in scores:
        r = []
        for s in pair:
            m = jnp.max(s, axis=-1, keepdims=True)
            e = jnp.exp(s - m)
            r.append((m, jnp.sum(e, axis=-1, keepdims=True), e.astype(BF16)))
        stats.append(r)

    nums = []
    for (gi, pr, _, k_row, nk, _, _), r in zip(items, stats):
        v2 = v_s[gi, pl.ds(k_row, nk), pr * LANES:(pr + 1) * LANES]
        nums.append(jnp.where(lo, _dot(r[0][2], v2), _dot(r[1][2], v2)))

    for (gi, pr, _, _, _, out_rows, mode), r, a2 in zip(items, stats, nums):
        m2 = jnp.where(lo, jnp.broadcast_to(r[0][0], shape), jnp.broadcast_to(r[1][0], shape))
        l2 = jnp.where(lo, jnp.broadcast_to(r[0][1], shape), jnp.broadcast_to(r[1][1], shape))
        if mode == "init":
            m_s[pr, out_rows, :] = m2
            l_s[pr, out_rows, :] = l2
            acc_s[pr, out_rows, :] = a2
        else:
            mo = m_s[pr, out_rows, :]
            mn = jnp.maximum(mo, m2)
            eo = jnp.exp(mo - mn)
            e2 = jnp.exp(m2 - mn)
            ln = eo * l_s[pr, out_rows, :] + e2 * l2
            an = eo * acc_s[pr, out_rows, :] + e2 * a2
            if mode == "merge":
                m_s[pr, out_rows, :] = mn
                l_s[pr, out_rows, :] = ln
                acc_s[pr, out_rows, :] = an
            else:
                oc_ref[out_rows, pr * LANES:(pr + 1) * LANES] = (an / ln).astype(BF16)


def _attn_kernel(x_ref, g_ref, w_ref, bias_ref, oc_ref,
                 q_s, k_s, v_s, stage, m_s, l_s, acc_s):
    g = g_ref[...]
    hw = ATT_OUT

    def proj_tile(tt, carry):
        r0 = pl.multiple_of(tt * PROJ_TILE, PROJ_TILE)
        h = _rms(x_ref[pl.ds(r0, PROJ_TILE), :], g).astype(BF16)
        for gi, (_, dil) in enumerate(DIL_GROUPS):
            z = _dot(h, w_ref[gi])
            if dil == 1:
                q_s[gi, pl.ds(r0, PROJ_TILE), :] = (z[:, :hw] * HEAD_DIM ** -0.5).astype(BF16)
                k_s[gi, pl.ds(r0, PROJ_TILE), :] = z[:, hw:2 * hw].astype(BF16)
                v_s[gi, pl.ds(r0, PROJ_TILE), :] = z[:, 2 * hw:].astype(BF16)
            else:
                nt = hw // LANES
                for j in range(3 * nt):
                    stage[j] = z[:, j * LANES:(j + 1) * LANES]
                n = PROJ_TILE // dil
                cls_len = SEQ // dil
                for c in range(dil):
                    dst = pl.multiple_of(c * cls_len + tt * n, n)
                    for j in range(3 * nt):
                        zz = stage[j, pl.ds(c, n, stride=dil), :]
                        cols = slice((j % nt) * LANES, (j % nt + 1) * LANES)
                        if j < nt:
                            q_s[gi, pl.ds(dst, n), cols] = (zz * HEAD_DIM ** -0.5).astype(BF16)
                        elif j < 2 * nt:
                            k_s[gi, pl.ds(dst, n), cols] = zz.astype(BF16)
                        else:
                            v_s[gi, pl.ds(dst, n), cols] = zz.astype(BF16)
        return carry

    lax.fori_loop(0, SEQ // PROJ_TILE, proj_tile, 0)

    refs = (q_s, k_s, v_s, bias_ref, m_s, l_s, acc_s, oc_ref)
    pairs = range(HEADS_PER_GROUP // 2)

    dil2 = DIL_GROUPS[2][1]

    def g2_body(i, carry):
        items = []
        for u in range(G2_UNROLL):
            c = i * G2_UNROLL + u
            row = pl.multiple_of(c * ATT_BLOCK, ATT_BLOCK)
            out_rows = pl.ds(c, ATT_BLOCK, stride=dil2)
            items += [(2, pr, row, row, ATT_BLOCK, out_rows, "init") for pr in pairs]
        _attend_items(items, *refs)
        return carry

    lax.fori_loop(0, dil2 // G2_UNROLL, g2_body, 0)

    dil1 = DIL_GROUPS[1][1]
    cls1 = SEQ // dil1
    for n in range(cls1 // ATT_BLOCK):
        items = []
        for c in range(dil1):
            row = c * cls1 + n * ATT_BLOCK
            out_rows = pl.ds(n * (ATT_BLOCK * dil1) + c, ATT_BLOCK, stride=dil1)
            if n == 0:
                items += [(1, pr, row, row, ATT_BLOCK, out_rows, "merge") for pr in pairs]
            else:
                items += [(1, pr, row, row - ATT_BLOCK, 2 * ATT_BLOCK, out_rows, "merge")
                          for pr in pairs]
        _attend_items(items, *refs)

    _attend_items([(0, pr, 0, 0, ATT_BLOCK, pl.ds(0, ATT_BLOCK), "final") for pr in pairs], *refs)

    def g0_body(i, carry):
        items = []
        for u in range(G0_UNROLL):
            row = pl.multiple_of((1 + i * G0_UNROLL + u) * ATT_BLOCK, ATT_BLOCK)
            prev = pl.multiple_of(row - ATT_BLOCK, ATT_BLOCK)
            items += [(0, pr, row, prev, 2 * ATT_BLOCK, pl.ds(row, ATT_BLOCK), "final")
                      for pr in pairs]
        _attend_items(items, *refs)
        return carry

    lax.fori_loop(0, (SEQ // ATT_BLOCK - 1) // G0_UNROLL, g0_body, 0)


def _attention(x, pre_g, w_qkv, bias):
    B = x.shape[0]
    ng = len(DIL_GROUPS)
    return pl.pallas_call(
        _attn_kernel,
        out_shape=jax.ShapeDtypeStruct((B, SEQ, ATT_OUT), BF16),
        grid=(B,),
        in_specs=[
            pl.BlockSpec((None, SEQ, D_MODEL), lambda b: (b, 0, 0)),
            _const_spec((1, D_MODEL)),
            _const_spec((ng, D_MODEL, 3 * ATT_OUT)),
            _const_spec((ATT_HEADS, ATT_BLOCK, 2 * ATT_BLOCK)),
        ],
        out_specs=pl.BlockSpec((None, SEQ, ATT_OUT), lambda b: (b, 0, 0)),
        scratch_shapes=[
            pltpu.VMEM((ng, SEQ, ATT_OUT), BF16),
            pltpu.VMEM((ng, SEQ, ATT_OUT), BF16),
            pltpu.VMEM((ng, SEQ, ATT_OUT), BF16),
            pltpu.VMEM((3 * ATT_OUT // LANES, PROJ_TILE, LANES), F32),
            pltpu.VMEM((ATT_OUT // LANES, SEQ, LANES), F32),
            pltpu.VMEM((ATT_OUT // LANES, SEQ, LANES), F32),
            pltpu.VMEM((ATT_OUT // LANES, SEQ, LANES), F32),
        ],
        compiler_params=pltpu.CompilerParams(
            dimension_semantics=("arbitrary",), vmem_limit_bytes=VMEM_LIMIT_BYTES),
        name="dilated_attention",
    )(x, pre_g, w_qkv, bias)


def _mixer_kernel(x_ref, oc_ref, pre_g, post_g, w_a, w_gb, b_gb,
                  conv_w, conv_b, ln_a_g, ln_a_b, w_a_out,
                  ln_b_g, ln_b_b, ws_ref, bs_ref, w_b_out, w_c_out, w_mix,
                  out_ref, abuf, cbuf, a2buf, zbuf):
    ts = SEQ_TILE
    s = pl.program_id(1)
    x = x_ref[...]
    h = _rms(x, pre_g[...]).astype(BF16)

    za = _dot(h, w_a[...])
    a = za[:, :CONV_CH] * _sigmoid(za[:, CONV_CH:])

    @pl.when(s == 0)
    def _():
        abuf[0:CONV_HALO, :] = jnp.zeros((CONV_HALO, CONV_CH), F32)

    @pl.when(s > 0)
    def _():
        abuf[0:CONV_HALO, :] = abuf[ts:ts + CONV_HALO, :]

    abuf[CONV_HALO:CONV_HALO + ts, :] = a
    lag = CONV_HALO - (CONV_K - 1)
    n_lt = CONV_CH // LANES
    n_rc = ts // CONV_ROWS
    n_zb = w_gb.shape[0]
    zb_per_step = n_zb // n_rc
    per_gate = D_MODEL // ZB_COLS

    def conv_and_project(rc):
        r0 = rc * CONV_ROWS
        for lt in range(n_lt):
            idx = rc * zb_per_step + lt
            z = _dot(h, w_gb[idx]) + b_gb[idx]
            zbuf[idx] = _sigmoid(z) if idx < N_BRANCH * per_gate else _gelu_tanh(z)
            cs = slice(lt * LANES, (lt + 1) * LANES)
            acc = jnp.broadcast_to(conv_b[:, cs], (CONV_ROWS, LANES))
            for r in range(SUBLANES):
                rows = CONV_ROWS + (SUBLANES if r else 0)
                part = None
                for j in range(CONV_K):
                    if (lag + j) % SUBLANES != r:
                        continue
                    off = r0 + (lag + j - r)
                    term = conv_w[j:j + 1, cs] * abuf[off:off + rows, cs]
                    part = term if part is None else part + term
                acc = acc + part[r:r + CONV_ROWS]
            cbuf[r0:r0 + CONV_ROWS, cs] = acc

    for rc in range(n_rc):
        pl.when(s < pl.num_programs(1))(functools.partial(conv_and_project, rc))

    def zcols(first, count):
        return jnp.concatenate([zbuf[first + i] for i in range(count)], axis=1)

    for rc in range(ts // CONV_ROWS):
        rs = slice(rc * CONV_ROWS, (rc + 1) * CONV_ROWS)
        y = _layer_norm(cbuf[rs, :], ln_a_g[...], ln_a_b[...])
        a2buf[rs, :] = (y * _sigmoid(y)).astype(BF16)
    y_a = _dot(a2buf[...], w_a_out[...])
    merged = zcols(0, per_gate) * y_a

    zb = zcols(N_BRANCH * per_gate, COL_B // ZB_COLS)
    u = zb[:, :SG_CH]
    v = _layer_norm(zb[:, SG_CH:], ln_b_g[...], ln_b_b[...]).astype(BF16)
    nch = ts // SG_CHUNK
    gw = SG_CH // SG_GROUPS
    ti = lax.broadcasted_iota(jnp.int32, (SG_CHUNK, SG_CHUNK), 0)
    si = lax.broadcasted_iota(jnp.int32, (SG_CHUNK, SG_CHUNK), 1)
    causal = si <= ti
    mixed = []
    for gi in range(SG_GROUPS):
        wsm = jnp.where(causal, ws_ref[gi], 0.0).astype(BF16)
        rhs = jnp.concatenate(
            [v[n * SG_CHUNK:(n + 1) * SG_CHUNK, gi * gw:(gi + 1) * gw] for n in range(nch)], axis=1)
        r = _dot(wsm, rhs)
        mixed.append(jnp.concatenate(
            [r[:, n * gw:(n + 1) * gw] for n in range(nch)], axis=0))
    bs_tile = jnp.concatenate([bs_ref[...]] * nch, axis=0)
    vb = jnp.concatenate(mixed, axis=1) + bs_tile
    y_b = _dot((u * vb).astype(BF16), w_b_out[...])
    merged = merged + zcols(per_gate, per_gate) * y_b

    y_c = _dot(oc_ref[...], w_c_out[...])
    merged = merged + zcols(2 * per_gate, per_gate) * y_c

    y = _dot(merged.astype(BF16), w_mix[...])
    out_ref[...] = x + _rms(y, post_g[...])


def _mixer(x, oc, pre_g, post_g, w_a, w_gb, b_gb, conv_w, conv_b, ln_a_g, ln_a_b, w_a_out,
           ln_b_g, ln_b_b, w_s, bs_cols, w_b_out, w_c_out, w_mix):
    B = x.shape[0]
    ts = SEQ_TILE
    consts = (pre_g, post_g, w_a, w_gb, b_gb, conv_w, conv_b, ln_a_g, ln_a_b, w_a_out,
              ln_b_g, ln_b_b, w_s, bs_cols, w_b_out, w_c_out, w_mix)
    return pl.pallas_call(
        _mixer_kernel,
        out_shape=jax.ShapeDtypeStruct(x.shape, F32),
        grid=(B, SEQ // ts),
        in_specs=[
            pl.BlockSpec((None, ts, D_MODEL), lambda b, s: (b, s, 0)),
            pl.BlockSpec((None, ts, ATT_OUT), lambda b, s: (b, s, 0)),
        ] + [_const_spec(c.shape) for c in consts],
        out_specs=pl.BlockSpec((None, ts, D_MODEL), lambda b, s: (b, s, 0)),
        scratch_shapes=[
            pltpu.VMEM((CONV_HALO + ts, CONV_CH), F32),
            pltpu.VMEM((ts, CONV_CH), F32),
            pltpu.VMEM((ts, CONV_CH), BF16),
            pltpu.VMEM(((N_BRANCH * D_MODEL + COL_B) // ZB_COLS, ts, ZB_COLS), F32),
        ],
        compiler_params=pltpu.CompilerParams(
            dimension_semantics=("arbitrary", "arbitrary"), vmem_limit_bytes=VMEM_LIMIT_BYTES),
        name="parallel_mixer",
    )(x, oc, *consts)


def _xattn_kernel(x_ref, mem_ref, pre_g, post_g, mem_g, w_q, w_kv, w_o, out_ref, k_s, v_s):
    s = pl.program_id(1)

    @pl.when(s == 0)
    def _():
        mn = _rms(mem_ref[...], mem_g[...]).astype(BF16)
        kv = _dot(mn, w_kv[...])
        k_s[...] = kv[:, :D_MODEL].astype(BF16)
        v_s[...] = kv[:, D_MODEL:].astype(BF16)

    x = x_ref[...]
    h = _rms(x, pre_g[...]).astype(BF16)
    q = (_dot(h, w_q[...]) * X_HEAD_DIM ** -0.5).astype(BF16)
    heads = [slice(hd * X_HEAD_DIM, (hd + 1) * X_HEAD_DIM) for hd in range(X_HEADS)]
    scores = [lax.dot_general(q[:, c], k_s[:, c], _NT, preferred_element_type=F32) for c in heads]
    probs = []
    for sc in scores:
        e = jnp.exp(sc - jnp.max(sc, axis=-1, keepdims=True))
        probs.append((e / jnp.sum(e, axis=-1, keepdims=True)).astype(BF16))
    outs = [_dot(p, v_s[:, c]) for p, c in zip(probs, heads)]
    o = jnp.concatenate(outs, axis=1).astype(BF16)
    y = _dot(o, w_o[...])
    out_ref[...] = x + _rms(y, post_g[...])


def _cross_attention(x, mem, pre_g, post_g, mem_g, w_q, w_kv, w_o):
    B = x.shape[0]
    ts = SEQ_TILE
    consts = (pre_g, post_g, mem_g, w_q, w_kv, w_o)
    return pl.pallas_call(
        _xattn_kernel,
        out_shape=jax.ShapeDtypeStruct(x.shape, F32),
        grid=(B, SEQ // ts),
        in_specs=[
            pl.BlockSpec((None, ts, D_MODEL), lambda b, s: (b, s, 0)),
            pl.BlockSpec((None, MEM_LEN, D_MODEL), lambda b, s: (b, 0, 0)),
        ] + [_const_spec(c.shape) for c in consts],
        out_specs=pl.BlockSpec((None, ts, D_MODEL), lambda b, s: (b, s, 0)),
        scratch_shapes=[
            pltpu.VMEM((MEM_LEN, D_MODEL), BF16),
            pltpu.VMEM((MEM_LEN, D_MODEL), BF16),
        ],
        compiler_params=pltpu.CompilerParams(
            dimension_semantics=("arbitrary", "arbitrary"), vmem_limit_bytes=VMEM_LIMIT_BYTES),
        name="memory_cross_attention",
    )(x, mem, *consts)


def _ffn_kernel(x_ref, pre_g, post_g, w_gate, w_val, conv_w, conv_b, w_down, out_ref, gbuf):
    ts = SEQ_TILE
    s = pl.program_id(1)
    x = x_ref[...]
    h = _rms(x, pre_g[...]).astype(BF16)

    @pl.when(s == 0)
    def _():
        gbuf[0:FFN_HALO, :] = jnp.zeros((FFN_HALO, D_FF), F32)

    @pl.when(s > 0)
    def _():
        gbuf[0:FFN_HALO, :] = gbuf[ts:ts + FFN_HALO, :]

    gbuf[FFN_HALO:FFN_HALO + ts, :] = _dot(h, w_gate[...])
    lag = FFN_HALO - (FFN_CONV_K - 1)
    gc = jnp.broadcast_to(conv_b[...], (ts, D_FF))
    for j in range(FFN_CONV_K):
        gc = gc + conv_w[j:j + 1, :] * gbuf[lag + j:lag + j + ts, :]
    act = _gelu_tanh(gc) * _dot(h, w_val[...])
    y = _dot(act.astype(BF16), w_down[...])
    out_ref[...] = x + _rms(y, post_g[...])


def _conv_ffn(x, pre_g, post_g, w_gate, w_val, conv_w, conv_b, w_down):
    B = x.shape[0]
    ts = SEQ_TILE
    consts = (pre_g, post_g, w_gate, w_val, conv_w, conv_b, w_down)
    return pl.pallas_call(
        _ffn_kernel,
        out_shape=jax.ShapeDtypeStruct(x.shape, F32),
        grid=(B, SEQ // ts),
        in_specs=[pl.BlockSpec((None, ts, D_MODEL), lambda b, s: (b, s, 0))]
        + [_const_spec(c.shape) for c in consts],
        out_specs=pl.BlockSpec((None, ts, D_MODEL), lambda b, s: (b, s, 0)),
        scratch_shapes=[pltpu.VMEM((FFN_HALO + ts, D_FF), F32)],
        compiler_params=pltpu.CompilerParams(
            dimension_semantics=("arbitrary", "arbitrary"), vmem_limit_bytes=VMEM_LIMIT_BYTES),
        name="conv_ffn",
    )(x, *consts)


def _row(v):
    return v.reshape(1, -1)


def _column_blocks(w_gate, w_b):
    w = jnp.concatenate([w_gate, w_b], axis=1).astype(BF16)
    return w.reshape(D_MODEL, -1, ZB_COLS).transpose(1, 0, 2)


def _bias_blocks(b_gate):
    b = jnp.concatenate([b_gate.reshape(-1), jnp.zeros((COL_B,), F32)])
    return b.reshape(-1, 1, ZB_COLS)


def _qkv_weights(w_c):
    n = ATT_HEADS * HEAD_DIM
    parts = []
    for gi in range(len(DIL_GROUPS)):
        cols = [w_c[:, t * n + gi * ATT_OUT:t * n + (gi + 1) * ATT_OUT] for t in range(3)]
        parts.append(jnp.concatenate(cols, axis=1))
    return jnp.stack(parts).astype(BF16)


def kernel(x, mem, rel_bias, mix_pre_g, mix_post_g, w_in, b_gate, conv_a_w, conv_a_b, ln_a_g, ln_a_b, w_a_out, ln_b_g, ln_b_b, w_s, b_s, w_b_out, w_c_out, w_mix_out, x_pre_g, x_post_g, mem_g, w_xq, w_xkv, w_xo, ffn_pre_g, ffn_post_g, w_up, conv_f_w, conv_f_b, w_down):
    assert x.shape[1:] == (SEQ, D_MODEL) and mem.shape[1:] == (MEM_LEN, D_MODEL)
    depth = w_in.shape[0]
    bias = _bias_tables(rel_bias)
    gw = SG_CH // SG_GROUPS
    for l in range(depth):
        w = w_in[l]
        oc = _attention(x, _row(mix_pre_g[l]), _qkv_weights(w[:, OFF_C:OFF_G]), bias)
        bs_cols = jnp.repeat(b_s[l].T, gw, axis=1)
        x = _mixer(
            x, oc, _row(mix_pre_g[l]), _row(mix_post_g[l]),
            w[:, :OFF_B].astype(BF16), _column_blocks(w[:, OFF_G:], w[:, OFF_B:OFF_C]),
            _bias_blocks(b_gate[l]), conv_a_w[l], _row(conv_a_b[l]), _row(ln_a_g[l]), _row(ln_a_b[l]),
            w_a_out[l].astype(BF16), _row(ln_b_g[l]), _row(ln_b_b[l]), w_s[l], bs_cols,
            w_b_out[l].astype(BF16), w_c_out[l].astype(BF16), w_mix_out[l].astype(BF16))
        x = _cross_attention(
            x, mem, _row(x_pre_g[l]), _row(x_post_g[l]), _row(mem_g[l]),
            w_xq[l].astype(BF16), w_xkv[l].astype(BF16), w_xo[l].astype(BF16))
        x = _conv_ffn(
            x, _row(ffn_pre_g[l]), _row(ffn_post_g[l]),
            w_up[l][:, :D_FF].astype(BF16), w_up[l][:, D_FF:].astype(BF16),
            conv_f_w[l], _row(conv_f_b[l]), w_down[l].astype(BF16))
    return x
```

```python
import functools
import math

import numpy as np
import jax
import jax.numpy as jnp
from jax import lax
from jax.experimental import pallas as pl
from jax.experimental.pallas import tpu as pltpu

D_MODEL = 1024
SEQ = 2048
MEM_LEN = 256
CONV_CH = 512
CONV_K = 31
SG_CH = 512
SG_GROUPS = 4
SG_CHUNK = 128
HEAD_DIM = 64
HEADS_PER_GROUP = 4
DIL_GROUPS = ((128, 1), (512, 4), (2048, 16))
ATT_HEADS = HEADS_PER_GROUP * len(DIL_GROUPS)
ATT_BLOCK = 128
REL_BUCKETS = 32
REL_MAX_DIST = 2048
N_BRANCH = 3
X_HEADS = 4
X_HEAD_DIM = D_MODEL // X_HEADS
D_FF = 2816
FFN_CONV_K = 3
NORM_EPS = 1e-6
LN_EPS = 1e-5

COL_A = 2 * CONV_CH
COL_B = 2 * SG_CH
COL_C = 3 * ATT_HEADS * HEAD_DIM
OFF_B = COL_A
OFF_C = COL_A + COL_B
OFF_G = COL_A + COL_B + COL_C
ATT_OUT = HEADS_PER_GROUP * HEAD_DIM

LANES = 128
SUBLANES = 8
VMEM_LIMIT_BYTES = 56 * 1024 * 1024
ATTN_VMEM_LIMIT_BYTES = 62 * 1024 * 1024

PROJ_TILE = 512
SEQ_TILE = 512
CONV_ROWS = 128
CONV_HALO = 32
FFN_HALO = 8
ZB_COLS = 256
G2_UNROLL = 4
G0_UNROLL = 3

BF16 = jnp.bfloat16
F32 = jnp.float32
_NT = (((1,), (1,)), ((), ()))


def _rms(x, g):
    return x * lax.rsqrt(jnp.mean(x * x, axis=-1, keepdims=True) + NORM_EPS) * g


def _layer_norm(x, g, b):
    mu = jnp.mean(x, axis=-1, keepdims=True)
    xc = x - mu
    var = jnp.mean(xc * xc, axis=-1, keepdims=True)
    return xc * lax.rsqrt(var + LN_EPS) * g + b


def _sigmoid(x):
    return 1.0 / (1.0 + jnp.exp(-x))


def _gelu_tanh(x):
    c = math.sqrt(2.0 / math.pi)
    return 0.5 * x * (1.0 + jnp.tanh(c * (x + 0.044715 * (x * x * x))))


def _dot(a, b):
    return jnp.dot(a, b, preferred_element_type=F32)


def _const_spec(shape):
    n = len(shape)
    return pl.BlockSpec(shape, lambda *_: (0,) * n, pipeline_mode=pl.Buffered(1))


def _bucket_tables():
    qi = np.arange(ATT_BLOCK)[:, None]
    ki = np.arange(2 * ATT_BLOCK)[None, :]
    rel = qi + ATT_BLOCK - ki
    max_exact = REL_BUCKETS // 2
    out = []
    for window, dil in DIL_GROUPS:
        span = window // dil
        dist = np.maximum(rel * dil, 0)
        nf = np.maximum(dist, 1)
        vals = []
        for dt in (np.float32, np.float64):
            x = np.log(nf.astype(dt) / dt(max_exact)) / dt(math.log(REL_MAX_DIST / max_exact))
            vals.append(max_exact + (x * dt(REL_BUCKETS - max_exact)).astype(np.int32))
        valid = (rel >= 0) & (rel <= span)
        assert np.array_equal(vals[0][valid], vals[1][valid])
        large = np.minimum(vals[0], REL_BUCKETS - 1)
        bucket = np.where(dist < max_exact, dist, large)
        out.append(np.where(valid, bucket, -1).astype(np.int32))
    return np.stack(out)


def _bias_kernel(tab_ref, bucket_ref, out_ref):
    for gi in range(len(DIL_GROUPS)):
        bk = bucket_ref[gi]
        for hh in range(HEADS_PER_GROUP):
            h = gi * HEADS_PER_GROUP + hh
            acc = jnp.full(bk.shape, -jnp.inf, F32)
            for b in range(REL_BUCKETS):
                acc = jnp.where(bk == b, tab_ref[b, h], acc)
            out_ref[h] = acc


def _bias_tables(rel_bias):
    buckets = jnp.asarray(_bucket_tables())
    return pl.pallas_call(
        _bias_kernel,
        out_shape=jax.ShapeDtypeStruct((ATT_HEADS, ATT_BLOCK, 2 * ATT_BLOCK), F32),
        in_specs=[pl.BlockSpec(memory_space=pltpu.SMEM),
                  pl.BlockSpec(memory_space=pltpu.VMEM)],
        out_specs=pl.BlockSpec(memory_space=pltpu.VMEM),
        name="rel_bias_tables",
    )(rel_bias, buckets)


def _attend_items(items, q_s, k_s, v_s, bias_ref, m_s, l_s, acc_s, oc_ref):
    lane = lax.broadcasted_iota(jnp.int32, (ATT_BLOCK, LANES), 1)
    lo = lane < HEAD_DIM
    shape = (ATT_BLOCK, LANES)

    scores = []
    for gi, pr, q_row, k_row, nk, _, _ in items:
        cols = slice(pr * LANES, (pr + 1) * LANES)
        q2 = q_s[gi, pl.ds(q_row, ATT_BLOCK), cols]
        k2 = k_s[gi, pl.ds(k_row, nk), cols]
        ha = gi * HEADS_PER_GROUP + 2 * pr
        if nk == ATT_BLOCK:
            ba = bias_ref[ha, :, ATT_BLOCK:]
            bb = bias_ref[ha + 1, :, ATT_BLOCK:]
        else:
            ba = bias_ref[ha]
            bb = bias_ref[ha + 1]
        zero = jnp.zeros_like(q2)
        sa = lax.dot_general(jnp.where(lo, q2, zero), k2, _NT, preferred_element_type=F32) + ba
        sb = lax.dot_general(jnp.where(lo, zero, q2), k2, _NT, preferred_element_type=F32) + bb
        scores.append((sa, sb))

    stats = []
    for pair in scores:
        r = []
        for s in pair:
            m = jnp.max(s, axis=-1, keepdims=True)
            e = jnp.exp(s - m)
            r.append((m, jnp.sum(e, axis=-1, keepdims=True), e.astype(BF16)))
        stats.append(r)

    nums = []
    for (gi, pr, _, k_row, nk, _, _), r in zip(items, stats):
        v2 = v_s[gi, pl.ds(k_row, nk), pr * LANES:(pr + 1) * LANES]
        nums.append(jnp.where(lo, _dot(r[0][2], v2), _dot(r[1][2], v2)))

    for (gi, pr, _, _, _, out_rows, mode), r, a2 in zip(items, stats, nums):
        m2 = jnp.where(lo, jnp.broadcast_to(r[0][0], shape), jnp.broadcast_to(r[1][0], shape))
        l2 = jnp.where(lo, jnp.broadcast_to(r[0][1], shape), jnp.broadcast_to(r[1][1], shape))
        if mode == "init":
            m_s[pr, out_rows, :] = m2
            l_s[pr, out_rows, :] = l2
            acc_s[pr, out_rows, :] = a2
        else:
            mo = m_s[pr, out_rows, :]
            mn = jnp.maximum(mo, m2)
            eo = jnp.exp(mo - mn)
            e2 = jnp.exp(m2 - mn)
            ln = eo * l_s[pr, out_rows, :] + e2 * l2
            an = eo * acc_s[pr, out_rows, :] + e2 * a2
            if mode == "merge":
                m_s[pr, out_rows, :] = mn
                l_s[pr, out_rows, :] = ln
                acc_s[pr, out_rows, :] = an
            else:
                oc_ref[out_rows, pr * LANES:(pr + 1) * LANES] = (an / ln).astype(BF16)


def _attn_kernel(x_ref, g_ref, w_ref, bias_ref, oc_ref,
                 q_s, k_s, v_s, stage, m_s, l_s, acc_s):
    g = g_ref[...]
    hw = ATT_OUT

    def proj_tile(tt, carry):
        r0 = tt * PROJ_TILE
        h = _rms(x_ref[pl.ds(r0, PROJ_TILE), :], g).astype(BF16)
        for gi, (_, dil) in enumerate(DIL_GROUPS):
            z = _dot(h, w_ref[gi])
            if dil == 1:
                q_s[gi, pl.ds(r0, PROJ_TILE), :] = (z[:, :hw] * HEAD_DIM ** -0.5).astype(BF16)
                k_s[gi, pl.ds(r0, PROJ_TILE), :] = z[:, hw:2 * hw].astype(BF16)
                v_s[gi, pl.ds(r0, PROJ_TILE), :] = z[:, 2 * hw:].astype(BF16)
            else:
                nt = hw // LANES
                for j in range(3 * nt):
                    stage[j] = z[:, j * LANES:(j + 1) * LANES]
                n = PROJ_TILE // dil
                cls_len = SEQ // dil
                for c in range(dil):
                    dst = c * cls_len + tt * n
                    for j in range(3 * nt):
                        zz = stage[j, pl.ds(c, n, stride=dil), :]
                        cols = slice((j % nt) * LANES, (j % nt + 1) * LANES)
                        if j < nt:
                            q_s[gi, pl.ds(dst, n), cols] = (zz * HEAD_DIM ** -0.5).astype(BF16)
                        elif j < 2 * nt:
                            k_s[gi, pl.ds(dst, n), cols] = zz.astype(BF16)
                        else:
                            v_s[gi, pl.ds(dst, n), cols] = zz.astype(BF16)
        return carry

    for tt in range(SEQ // PROJ_TILE):
        proj_tile(tt, 0)

    refs = (q_s, k_s, v_s, bias_ref, m_s, l_s, acc_s, oc_ref)
    pairs = range(HEADS_PER_GROUP // 2)

    dil2 = DIL_GROUPS[2][1]

    def g2_body(i, carry):
        items = []
        for u in range(G2_UNROLL):
            c = i * G2_UNROLL + u
            row = c * ATT_BLOCK
            out_rows = pl.ds(c, ATT_BLOCK, stride=dil2)
            items += [(2, pr, row, row, ATT_BLOCK, out_rows, "init") for pr in pairs]
        _attend_items(items, *refs)
        return carry

    for i in range(dil2 // G2_UNROLL):
        g2_body(i, 0)

    dil1 = DIL_GROUPS[1][1]
    cls1 = SEQ // dil1
    for n in range(cls1 // ATT_BLOCK):
        items = []
        for c in range(dil1):
            row = c * cls1 + n * ATT_BLOCK
            out_rows = pl.ds(n * (ATT_BLOCK * dil1) + c, ATT_BLOCK, stride=dil1)
            if n == 0:
                items += [(1, pr, row, row, ATT_BLOCK, out_rows, "merge") for pr in pairs]
            else:
                items += [(1, pr, row, row - ATT_BLOCK, 2 * ATT_BLOCK, out_rows, "merge")
                          for pr in pairs]
        _attend_items(items, *refs)

    _attend_items([(0, pr, 0, 0, ATT_BLOCK, pl.ds(0, ATT_BLOCK), "final") for pr in pairs], *refs)

    def g0_body(i, carry):
        items = []
        for u in range(G0_UNROLL):
            row = (1 + i * G0_UNROLL + u) * ATT_BLOCK
            prev = row - ATT_BLOCK
            items += [(0, pr, row, prev, 2 * ATT_BLOCK, pl.ds(row, ATT_BLOCK), "final")
                      for pr in pairs]
        _attend_items(items, *refs)
        return carry

    for i in range((SEQ // ATT_BLOCK - 1) // G0_UNROLL):
        g0_body(i, 0)


def _attention(x, pre_g, w_qkv, bias):
    B = x.shape[0]
    ng = len(DIL_GROUPS)
    return pl.pallas_call(
        _attn_kernel,
        out_shape=jax.ShapeDtypeStruct((B, SEQ, ATT_OUT), BF16),
        grid=(B,),
        in_specs=[
            pl.BlockSpec((None, SEQ, D_MODEL), lambda b: (b, 0, 0)),
            _const_spec((1, D_MODEL)),
            _const_spec((ng, D_MODEL, 3 * ATT_OUT)),
            _const_spec((ATT_HEADS, ATT_BLOCK, 2 * ATT_BLOCK)),
        ],
        out_specs=pl.BlockSpec((None, SEQ, ATT_OUT), lambda b: (b, 0, 0)),
        scratch_shapes=[
            pltpu.VMEM((ng, SEQ, ATT_OUT), BF16),
            pltpu.VMEM((ng, SEQ, ATT_OUT), BF16),
            pltpu.VMEM((ng, SEQ, ATT_OUT), BF16),
            pltpu.VMEM((3 * ATT_OUT // LANES, PROJ_TILE, LANES), F32),
            pltpu.VMEM((ATT_OUT // LANES, SEQ, LANES), F32),
            pltpu.VMEM((ATT_OUT // LANES, SEQ, LANES), F32),
            pltpu.VMEM((ATT_OUT // LANES, SEQ, LANES), F32),
        ],
        compiler_params=pltpu.CompilerParams(
            dimension_semantics=("arbitrary",), vmem_limit_bytes=ATTN_VMEM_LIMIT_BYTES),
        name="dilated_attention",
    )(x, pre_g, w_qkv, bias)


def _mixer_kernel(x_ref, oc_ref, pre_g, post_g, w_a, w_gb, b_gb,
                  conv_w, conv_b, ln_a_g, ln_a_b, w_a_out,
                  ln_b_g, ln_b_b, ws_ref, bs_ref, w_b_out, w_c_out, w_mix,
                  out_ref, abuf, cbuf, a2buf, zbuf):
    ts = SEQ_TILE
    s = pl.program_id(1)
    x = x_ref[...]
    h = _rms(x, pre_g[...]).astype(BF16)

    za = _dot(h, w_a[...])
    a = za[:, :CONV_CH] * _sigmoid(za[:, CONV_CH:])

    @pl.when(s == 0)
    def _():
        abuf[0:CONV_HALO, :] = jnp.zeros((CONV_HALO, CONV_CH), F32)

    @pl.when(s > 0)
    def _():
        abuf[0:CONV_HALO, :] = abuf[ts:ts + CONV_HALO, :]

    abuf[CONV_HALO:CONV_HALO + ts, :] = a
    lag = CONV_HALO - (CONV_K - 1)
    n_lt = CONV_CH // LANES
    n_rc = ts // CONV_ROWS
    n_zb = w_gb.shape[0]
    zb_per_step = n_zb // n_rc
    per_gate = D_MODEL // ZB_COLS

    def conv_and_project(rc):
        r0 = rc * CONV_ROWS
        for lt in range(n_lt):
            idx = rc * zb_per_step + lt
            z = _dot(h, w_gb[idx]) + b_gb[idx]
            zbuf[idx] = _sigmoid(z) if idx < N_BRANCH * per_gate else _gelu_tanh(z)
            cs = slice(lt * LANES, (lt + 1) * LANES)
            acc = jnp.broadcast_to(conv_b[:, cs], (CONV_ROWS, LANES))
            for r in range(SUBLANES):
                rows = CONV_ROWS + (SUBLANES if r else 0)
                part = None
                for j in range(CONV_K):
                    if (lag + j) % SUBLANES != r:
                        continue
                    off = r0 + (lag + j - r)
                    term = conv_w[j:j + 1, cs] * abuf[off:off + rows, cs]
                    part = term if part is None else part + term
                acc = acc + part[r:r + CONV_ROWS]
            cbuf[r0:r0 + CONV_ROWS, cs] = acc

    for rc in range(n_rc):
        pl.when(s < pl.num_programs(1))(functools.partial(conv_and_project, rc))

    def zcols(first, count):
        return jnp.concatenate([zbuf[first + i] for i in range(count)], axis=1)

    for rc in range(ts // CONV_ROWS):
        rs = slice(rc * CONV_ROWS, (rc + 1) * CONV_ROWS)
        y = _layer_norm(cbuf[rs, :], ln_a_g[...], ln_a_b[...])
        a2buf[rs, :] = (y * _sigmoid(y)).astype(BF16)
    y_a = _dot(a2buf[...], w_a_out[...])
    merged = zcols(0, per_gate) * y_a

    zb = zcols(N_BRANCH * per_gate, COL_B // ZB_COLS)
    u = zb[:, :SG_CH]
    v = _layer_norm(zb[:, SG_CH:], ln_b_g[...], ln_b_b[...]).astype(BF16)
    nch = ts // SG_CHUNK
    gw = SG_CH // SG_GROUPS
    ti = lax.broadcasted_iota(jnp.int32, (SG_CHUNK, SG_CHUNK), 0)
    si = lax.broadcasted_iota(jnp.int32, (SG_CHUNK, SG_CHUNK), 1)
    causal = si <= ti
    mixed = []
    for gi in range(SG_GROUPS):
        wsm = jnp.where(causal, ws_ref[gi], 0.0).astype(BF16)
        rhs = jnp.concatenate(
            [v[n * SG_CHUNK:(n + 1) * SG_CHUNK, gi * gw:(gi + 1) * gw] for n in range(nch)], axis=1)
        r = _dot(wsm, rhs)
        mixed.append(jnp.concatenate(
            [r[:, n * gw:(n + 1) * gw] for n in range(nch)], axis=0))
    bs_tile = jnp.concatenate([bs_ref[...]] * nch, axis=0)
    vb = jnp.concatenate(mixed, axis=1) + bs_tile
    y_b = _dot((u * vb).astype(BF16), w_b_out[...])
    merged = merged + zcols(per_gate, per_gate) * y_b

    y_c = _dot(oc_ref[...], w_c_out[...])
    merged = merged + zcols(2 * per_gate, per_gate) * y_c

    y = _dot(merged.astype(BF16), w_mix[...])
    out_ref[...] = x + _rms(y, post_g[...])


def _mixer(x, oc, pre_g, post_g, w_a, w_gb, b_gb, conv_w, conv_b, ln_a_g, ln_a_b, w_a_out,
           ln_b_g, ln_b_b, w_s, bs_cols, w_b_out, w_c_out, w_mix):
    B = x.shape[0]
    ts = SEQ_TILE
    consts = (pre_g, post_g, w_a, w_gb, b_gb, conv_w, conv_b, ln_a_g, ln_a_b, w_a_out,
              ln_b_g, ln_b_b, w_s, bs_cols, w_b_out, w_c_out, w_mix)
    return pl.pallas_call(
        _mixer_kernel,
        out_shape=jax.ShapeDtypeStruct(x.shape, F32),
        grid=(B, SEQ // ts),
        in_specs=[
            pl.BlockSpec((None, ts, D_MODEL), lambda b, s: (b, s, 0)),
            pl.BlockSpec((None, ts, ATT_OUT), lambda b, s: (b, s, 0)),
        ] + [_const_spec(c.shape) for c in consts],
        out_specs=pl.BlockSpec((None, ts, D_MODEL), lambda b, s: (b, s, 0)),
        scratch_shapes=[
            pltpu.VMEM((CONV_HALO + ts, CONV_CH), F32),
            pltpu.VMEM((ts, CONV_CH), F32),
            pltpu.VMEM((ts, CONV_CH), BF16),
            pltpu.VMEM(((N_BRANCH * D_MODEL + COL_B) // ZB_COLS, ts, ZB_COLS), F32),
        ],
        compiler_params=pltpu.CompilerParams(
            dimension_semantics=("arbitrary", "arbitrary"), vmem_limit_bytes=VMEM_LIMIT_BYTES),
        name="parallel_mixer",
    )(x, oc, *consts)


def _xattn_kernel(x_ref, mem_ref, pre_g, post_g, mem_g, w_q, w_kv, w_o, out_ref, k_s, v_s):
    s = pl.program_id(1)

    @pl.when(s == 0)
    def _():
        mn = _rms(mem_ref[...], mem_g[...]).astype(BF16)
        kv = _dot(mn, w_kv[...])
        k_s[...] = kv[:, :D_MODEL].astype(BF16)
        v_s[...] = kv[:, D_MODEL:].astype(BF16)

    x = x_ref[...]
    h = _rms(x, pre_g[...]).astype(BF16)
    q = (_dot(h, w_q[...]) * X_HEAD_DIM ** -0.5).astype(BF16)
    heads = [slice(hd * X_HEAD_DIM, (hd + 1) * X_HEAD_DIM) for hd in range(X_HEADS)]
    scores = [lax.dot_general(q[:, c], k_s[:, c], _NT, preferred_element_type=F32) for c in heads]
    probs = []
    for sc in scores:
        e = jnp.exp(sc - jnp.max(sc, axis=-1, keepdims=True))
        probs.append((e / jnp.sum(e, axis=-1, keepdims=True)).astype(BF16))
    outs = [_dot(p, v_s[:, c]) for p, c in zip(probs, heads)]
    o = jnp.concatenate(outs, axis=1).astype(BF16)
    y = _dot(o, w_o[...])
    out_ref[...] = x + _rms(y, post_g[...])


def _cross_attention(x, mem, pre_g, post_g, mem_g, w_q, w_kv, w_o):
    B = x.shape[0]
    ts = SEQ_TILE
    consts = (pre_g, post_g, mem_g, w_q, w_kv, w_o)
    return pl.pallas_call(
        _xattn_kernel,
        out_shape=jax.ShapeDtypeStruct(x.shape, F32),
        grid=(B, SEQ // ts),
        in_specs=[
            pl.BlockSpec((None, ts, D_MODEL), lambda b, s: (b, s, 0)),
            pl.BlockSpec((None, MEM_LEN, D_MODEL), lambda b, s: (b, 0, 0)),
        ] + [_const_spec(c.shape) for c in consts],
        out_specs=pl.BlockSpec((None, ts, D_MODEL), lambda b, s: (b, s, 0)),
        scratch_shapes=[
            pltpu.VMEM((MEM_LEN, D_MODEL), BF16),
            pltpu.VMEM((MEM_LEN, D_MODEL), BF16),
        ],
        compiler_params=pltpu.CompilerParams(
            dimension_semantics=("arbitrary", "arbitrary"), vmem_limit_bytes=VMEM_LIMIT_BYTES),
        name="memory_cross_attention",
    )(x, mem, *consts)


def _ffn_kernel(x_ref, pre_g, post_g, w_gate, w_val, conv_w, conv_b, w_down, out_ref, gbuf):
    ts = SEQ_TILE
    s = pl.program_id(1)
    x = x_ref[...]
    h = _rms(x, pre_g[...]).astype(BF16)

    @pl.when(s == 0)
    def _():
        gbuf[0:FFN_HALO, :] = jnp.zeros((FFN_HALO, D_FF), F32)

    @pl.when(s > 0)
    def _():
        gbuf[0:FFN_HALO, :] = gbuf[ts:ts + FFN_HALO, :]

    gbuf[FFN_HALO:FFN_HALO + ts, :] = _dot(h, w_gate[...])
    lag = FFN_HALO - (FFN_CONV_K - 1)
    gc = jnp.broadcast_to(conv_b[...], (ts, D_FF))
    for j in range(FFN_CONV_K):
        gc = gc + conv_w[j:j + 1, :] * gbuf[lag + j:lag + j + ts, :]
    act = _gelu_tanh(gc) * _dot(h, w_val[...])
    y = _dot(act.astype(BF16), w_down[...])
    out_ref[...] = x + _rms(y, post_g[...])


def _conv_ffn(x, pre_g, post_g, w_gate, w_val, conv_w, conv_b, w_down):
    B = x.shape[0]
    ts = SEQ_TILE
    consts = (pre_g, post_g, w_gate, w_val, conv_w, conv_b, w_down)
    return pl.pallas_call(
        _ffn_kernel,
        out_shape=jax.ShapeDtypeStruct(x.shape, F32),
        grid=(B, SEQ // ts),
        in_specs=[pl.BlockSpec((None, ts, D_MODEL), lambda b, s: (b, s, 0))]
        + [_const_spec(c.shape) for c in consts],
        out_specs=pl.BlockSpec((None, ts, D_MODEL), lambda b, s: (b, s, 0)),
        scratch_shapes=[pltpu.VMEM((FFN_HALO + ts, D_FF), F32)],
        compiler_params=pltpu.CompilerParams(
            dimension_semantics=("arbitrary", "arbitrary"), vmem_limit_bytes=VMEM_LIMIT_BYTES),
        name="conv_ffn",
    )(x, *consts)


def _row(v):
    return v.reshape(1, -1)


def _column_blocks(w_gate, w_b):
    w = jnp.concatenate([w_gate, w_b], axis=1).astype(BF16)
    return w.reshape(D_MODEL, -1, ZB_COLS).transpose(1, 0, 2)


def _bias_blocks(b_gate):
    b = jnp.concatenate([b_gate.reshape(-1), jnp.zeros((COL_B,), F32)])
    return b.reshape(-1, 1, ZB_COLS)


def _qkv_weights(w_c):
    n = ATT_HEADS * HEAD_DIM
    parts = []
    for gi in range(len(DIL_GROUPS)):
        cols = [w_c[:, t * n + gi * ATT_OUT:t * n + (gi + 1) * ATT_OUT] for t in range(3)]
        parts.append(jnp.concatenate(cols, axis=1))
    return jnp.stack(parts).astype(BF16)


def kernel(x, mem, rel_bias, mix_pre_g, mix_post_g, w_in, b_gate, conv_a_w, conv_a_b, ln_a_g, ln_a_b, w_a_out, ln_b_g, ln_b_b, w_s, b_s, w_b_out, w_c_out, w_mix_out, x_pre_g, x_post_g, mem_g, w_xq, w_xkv, w_xo, ffn_pre_g, ffn_post_g, w_up, conv_f_w, conv_f_b, w_down):
    assert x.shape[1:] == (SEQ, D_MODEL) and mem.shape[1:] == (MEM_LEN, D_MODEL)
    depth = w_in.shape[0]
    bias = _bias_tables(rel_bias)
    gw = SG_CH // SG_GROUPS
    for l in range(depth):
        w = w_in[l]
        oc = _attention(x, _row(mix_pre_g[l]), _qkv_weights(w[:, OFF_C:OFF_G]), bias)
        bs_cols = jnp.repeat(b_s[l].T, gw, axis=1)
        x = _mixer(
            x, oc, _row(mix_pre_g[l]), _row(mix_post_g[l]),
            w[:, :OFF_B].astype(BF16), _column_blocks(w[:, OFF_G:], w[:, OFF_B:OFF_C]),
            _bias_blocks(b_gate[l]), conv_a_w[l], _row(conv_a_b[l]), _row(ln_a_g[l]), _row(ln_a_b[l]),
            w_a_out[l].astype(BF16), _row(ln_b_g[l]), _row(ln_b_b[l]), w_s[l], bs_cols,
            w_b_out[l].astype(BF16), w_c_out[l].astype(BF16), w_mix_out[l].astype(BF16))
        x = _cross_attention(
            x, mem, _row(x_pre_g[l]), _row(x_post_g[l]), _row(mem_g[l]),
            w_xq[l].astype(BF16), w_xkv[l].astype(BF16), w_xo[l].astype(BF16))
        x = _conv_ffn(
            x, _row(ffn_pre_g[l]), _row(ffn_post_g[l]),
            w_up[l][:, :D_FF].astype(BF16), w_up[l][:, D_FF:].astype(BF16),
            conv_f_w[l], _row(conv_f_b[l]), w_down[l].astype(BF16))
    return x
```

```python
import math

import numpy as np
import jax
import jax.numpy as jnp
from jax import lax
from jax.experimental import pallas as pl
from jax.experimental.pallas import tpu as pltpu

D_MODEL = 1024
SEQ = 2048
MEM_LEN = 256
CONV_CH = 512
CONV_K = 31
SG_CH = 512
SG_GROUPS = 4
SG_CHUNK = 128
HEAD_DIM = 64
HEADS_PER_GROUP = 4
DIL_GROUPS = ((128, 1), (512, 4), (2048, 16))
ATT_HEADS = HEADS_PER_GROUP * len(DIL_GROUPS)
ATT_BLOCK = 128
REL_BUCKETS = 32
REL_MAX_DIST = 2048
N_BRANCH = 3
X_HEADS = 4
X_HEAD_DIM = D_MODEL // X_HEADS
D_FF = 2816
FFN_CONV_K = 3
NORM_EPS = 1e-6
LN_EPS = 1e-5

COL_A = 2 * CONV_CH
COL_B = 2 * SG_CH
COL_C = 3 * ATT_HEADS * HEAD_DIM
OFF_B = COL_A
OFF_C = COL_A + COL_B
OFF_G = COL_A + COL_B + COL_C
ATT_OUT = HEADS_PER_GROUP * HEAD_DIM

LANES = 128
SUBLANES = 8
VMEM_LIMIT_BYTES = 56 * 1024 * 1024
ATTN_VMEM_LIMIT_BYTES = 62 * 1024 * 1024

PROJ_TILE = 512
SEQ_TILE = 512
CONV_ROWS = 128
CONV_HALO = 32
FFN_HALO = 8
ZB_COLS = 256
XATTN_SUBTILES = 4
G2_UNROLL = 4
G0_UNROLL = 3

BF16 = jnp.bfloat16
F32 = jnp.float32
_NT = (((1,), (1,)), ((), ()))


def _rms(x, g):
    return x * lax.rsqrt(jnp.mean(x * x, axis=-1, keepdims=True) + NORM_EPS) * g


def _layer_norm(x, g, b):
    mu = jnp.mean(x, axis=-1, keepdims=True)
    xc = x - mu
    var = jnp.mean(xc * xc, axis=-1, keepdims=True)
    return xc * lax.rsqrt(var + LN_EPS) * g + b


def _sigmoid(x):
    return 1.0 / (1.0 + jnp.exp(-x))


def _gelu_tanh(x):
    c = math.sqrt(2.0 / math.pi)
    return 0.5 * x * (1.0 + jnp.tanh(c * (x + 0.044715 * (x * x * x))))


def _dot(a, b):
    return jnp.dot(a, b, preferred_element_type=F32)


def _const_spec(shape):
    n = len(shape)
    return pl.BlockSpec(shape, lambda *_: (0,) * n, pipeline_mode=pl.Buffered(1))


def _bucket_tables():
    qi = np.arange(ATT_BLOCK)[:, None]
    ki = np.arange(2 * ATT_BLOCK)[None, :]
    rel = qi + ATT_BLOCK - ki
    max_exact = REL_BUCKETS // 2
    out = []
    for window, dil in DIL_GROUPS:
        span = window // dil
        dist = np.maximum(rel * dil, 0)
        nf = np.maximum(dist, 1)
        vals = []
        for dt in (np.float32, np.float64):
            x = np.log(nf.astype(dt) / dt(max_exact)) / dt(math.log(REL_MAX_DIST / max_exact))
            vals.append(max_exact + (x * dt(REL_BUCKETS - max_exact)).astype(np.int32))
        valid = (rel >= 0) & (rel <= span)
        assert np.array_equal(vals[0][valid], vals[1][valid])
        large = np.minimum(vals[0], REL_BUCKETS - 1)
        bucket = np.where(dist < max_exact, dist, large)
        out.append(np.where(valid, bucket, -1).astype(np.int32))
    return np.stack(out)


def _bias_kernel(tab_ref, bucket_ref, out_ref):
    for gi in range(len(DIL_GROUPS)):
        bk = bucket_ref[gi]
        for hh in range(HEADS_PER_GROUP):
            h = gi * HEADS_PER_GROUP + hh
            acc = jnp.full(bk.shape, -jnp.inf, F32)
            for b in range(REL_BUCKETS):
                acc = jnp.where(bk == b, tab_ref[b, h], acc)
            out_ref[h] = acc


def _bias_tables(rel_bias):
    buckets = jnp.asarray(_bucket_tables())
    return pl.pallas_call(
        _bias_kernel,
        out_shape=jax.ShapeDtypeStruct((ATT_HEADS, ATT_BLOCK, 2 * ATT_BLOCK), F32),
        in_specs=[pl.BlockSpec(memory_space=pltpu.SMEM),
                  pl.BlockSpec(memory_space=pltpu.VMEM)],
        out_specs=pl.BlockSpec(memory_space=pltpu.VMEM),
        name="rel_bias_tables",
    )(rel_bias, buckets)


def _attend_items(items, q_s, k_s, v_s, bias_ref, m_s, l_s, acc_s, oc_ref):
    lane = lax.broadcasted_iota(jnp.int32, (ATT_BLOCK, LANES), 1)
    lo = lane < HEAD_DIM
    shape = (ATT_BLOCK, LANES)

    scores = []
    for gi, pr, q_row, k_row, nk, _, _ in items:
        cols = slice(pr * LANES, (pr + 1) * LANES)
        q2 = q_s[gi, pl.ds(q_row, ATT_BLOCK), cols]
        k2 = k_s[gi, pl.ds(k_row, nk), cols]
        ha = gi * HEADS_PER_GROUP + 2 * pr
        if nk == ATT_BLOCK:
            ba = bias_ref[ha, :, ATT_BLOCK:]
            bb = bias_ref[ha + 1, :, ATT_BLOCK:]
        else:
            ba = bias_ref[ha]
            bb = bias_ref[ha + 1]
        zero = jnp.zeros_like(q2)
        sa = lax.dot_general(jnp.where(lo, q2, zero), k2, _NT, preferred_element_type=F32) + ba
        sb = lax.dot_general(jnp.where(lo, zero, q2), k2, _NT, preferred_element_type=F32) + bb
        scores.append((sa, sb))

    stats = []
    for pair in scores:
        r = []
        for s in pair:
            m = jnp.max(s, axis=-1, keepdims=True)
            e = jnp.exp(s - m)
            r.append((m, jnp.sum(e, axis=-1, keepdims=True), e.astype(BF16)))
        stats.append(r)

    nums = []
    for (gi, pr, _, k_row, nk, _, _), r in zip(items, stats):
        v2 = v_s[gi, pl.ds(k_row, nk), pr * LANES:(pr + 1) * LANES]
        nums.append(jnp.where(lo, _dot(r[0][2], v2), _dot(r[1][2], v2)))

    for (gi, pr, _, _, _, out_rows, mode), r, a2 in zip(items, stats, nums):
        m2 = jnp.where(lo, jnp.broadcast_to(r[0][0], shape), jnp.broadcast_to(r[1][0], shape))
        l2 = jnp.where(lo, jnp.broadcast_to(r[0][1], shape), jnp.broadcast_to(r[1][1], shape))
        if mode == "init":
            m_s[pr, out_rows, :] = m2
            l_s[pr, out_rows, :] = l2
            acc_s[pr, out_rows, :] = a2
        else:
            mo = m_s[pr, out_rows, :]
            mn = jnp.maximum(mo, m2)
            eo = jnp.exp(mo - mn)
            e2 = jnp.exp(m2 - mn)
            ln = eo * l_s[pr, out_rows, :] + e2 * l2
            an = eo * acc_s[pr, out_rows, :] + e2 * a2
            if mode == "merge":
                m_s[pr, out_rows, :] = mn
                l_s[pr, out_rows, :] = ln
                acc_s[pr, out_rows, :] = an
            else:
                oc_ref[out_rows, pr * LANES:(pr + 1) * LANES] = (an / ln).astype(BF16)


def _attn_kernel(x_ref, g_ref, w_ref, bias_ref, oc_ref,
                 q_s, k_s, v_s, stage, m_s, l_s, acc_s):
    g = g_ref[...]
    hw = ATT_OUT

    def proj_tile(tt, carry):
        r0 = tt * PROJ_TILE
        h = _rms(x_ref[pl.ds(r0, PROJ_TILE), :], g).astype(BF16)
        for gi, (_, dil) in enumerate(DIL_GROUPS):
            z = _dot(h, w_ref[gi])
            if dil == 1:
                q_s[gi, pl.ds(r0, PROJ_TILE), :] = (z[:, :hw] * HEAD_DIM ** -0.5).astype(BF16)
                k_s[gi, pl.ds(r0, PROJ_TILE), :] = z[:, hw:2 * hw].astype(BF16)
                v_s[gi, pl.ds(r0, PROJ_TILE), :] = z[:, 2 * hw:].astype(BF16)
            else:
                nt = hw // LANES
                for j in range(3 * nt):
                    stage[j] = z[:, j * LANES:(j + 1) * LANES]
                n = PROJ_TILE // dil
                cls_len = SEQ // dil
                for c in range(dil):
                    dst = c * cls_len + tt * n
                    for j in range(3 * nt):
                        zz = stage[j, pl.ds(c, n, stride=dil), :]
                        cols = slice((j % nt) * LANES, (j % nt + 1) * LANES)
                        if j < nt:
                            q_s[gi, pl.ds(dst, n), cols] = (zz * HEAD_DIM ** -0.5).astype(BF16)
                        elif j < 2 * nt:
                            k_s[gi, pl.ds(dst, n), cols] = zz.astype(BF16)
                        else:
                            v_s[gi, pl.ds(dst, n), cols] = zz.astype(BF16)
        return carry

    for tt in range(SEQ // PROJ_TILE):
        proj_tile(tt, 0)

    refs = (q_s, k_s, v_s, bias_ref, m_s, l_s, acc_s, oc_ref)
    pairs = range(HEADS_PER_GROUP // 2)

    dil2 = DIL_GROUPS[2][1]

    def g2_body(i, carry):
        items = []
        for u in range(G2_UNROLL):
            c = i * G2_UNROLL + u
            row = c * ATT_BLOCK
            out_rows = pl.ds(c, ATT_BLOCK, stride=dil2)
            items += [(2, pr, row, row, ATT_BLOCK, out_rows, "init") for pr in pairs]
        _attend_items(items, *refs)
        return carry

    for i in range(dil2 // G2_UNROLL):
        g2_body(i, 0)

    dil1 = DIL_GROUPS[1][1]
    cls1 = SEQ // dil1
    for n in range(cls1 // ATT_BLOCK):
        items = []
        for c in range(dil1):
            row = c * cls1 + n * ATT_BLOCK
            out_rows = pl.ds(n * (ATT_BLOCK * dil1) + c, ATT_BLOCK, stride=dil1)
            if n == 0:
                items += [(1, pr, row, row, ATT_BLOCK, out_rows, "merge") for pr in pairs]
            else:
                items += [(1, pr, row, row - ATT_BLOCK, 2 * ATT_BLOCK, out_rows, "merge")
                          for pr in pairs]
        _attend_items(items, *refs)

    _attend_items([(0, pr, 0, 0, ATT_BLOCK, pl.ds(0, ATT_BLOCK), "final") for pr in pairs], *refs)

    def g0_body(i, carry):
        items = []
        for u in range(G0_UNROLL):
            row = (1 + i * G0_UNROLL + u) * ATT_BLOCK
            prev = row - ATT_BLOCK
            items += [(0, pr, row, prev, 2 * ATT_BLOCK, pl.ds(row, ATT_BLOCK), "final")
                      for pr in pairs]
        _attend_items(items, *refs)
        return carry

    for i in range((SEQ // ATT_BLOCK - 1) // G0_UNROLL):
        g0_body(i, 0)


def _attention(x, pre_g, w_qkv, bias):
    B = x.shape[0]
    ng = len(DIL_GROUPS)
    return pl.pallas_call(
        _attn_kernel,
        out_shape=jax.ShapeDtypeStruct((B, SEQ, ATT_OUT), BF16),
        grid=(B,),
        in_specs=[
            pl.BlockSpec((None, SEQ, D_MODEL), lambda b: (b, 0, 0)),
            _const_spec((1, D_MODEL)),
            _const_spec((ng, D_MODEL, 3 * ATT_OUT)),
            _const_spec((ATT_HEADS, ATT_BLOCK, 2 * ATT_BLOCK)),
        ],
        out_specs=pl.BlockSpec((None, SEQ, ATT_OUT), lambda b: (b, 0, 0)),
        scratch_shapes=[
            pltpu.VMEM((ng, SEQ, ATT_OUT), BF16),
            pltpu.VMEM((ng, SEQ, ATT_OUT), BF16),
            pltpu.VMEM((ng, SEQ, ATT_OUT), BF16),
            pltpu.VMEM((3 * ATT_OUT // LANES, PROJ_TILE, LANES), F32),
            pltpu.VMEM((ATT_OUT // LANES, SEQ, LANES), F32),
            pltpu.VMEM((ATT_OUT // LANES, SEQ, LANES), F32),
            pltpu.VMEM((ATT_OUT // LANES, SEQ, LANES), F32),
        ],
        compiler_params=pltpu.CompilerParams(
            dimension_semantics=("arbitrary",), vmem_limit_bytes=ATTN_VMEM_LIMIT_BYTES),
        name="dilated_attention",
    )(x, pre_g, w_qkv, bias)


def _mixer_kernel(x_ref, oc_ref, pre_g, post_g, w_a, w_gb, b_gb,
                  conv_w, conv_b, ln_a_g, ln_a_b, w_a_out,
                  ln_b_g, ln_b_b, ws_ref, bs_ref, w_b_out, w_c_out, w_mix,
                  out_ref, abuf, cbuf, a2buf, zbuf):
    ts = SEQ_TILE
    s = pl.program_id(1)
    x = x_ref[...]
    h = _rms(x, pre_g[...]).astype(BF16)

    za = _dot(h, w_a[...])
    a = za[:, :CONV_CH] * _sigmoid(za[:, CONV_CH:])

    @pl.when(s == 0)
    def _():
        abuf[0:CONV_HALO, :] = jnp.zeros((CONV_HALO, CONV_CH), F32)

    @pl.when(s > 0)
    def _():
        abuf[0:CONV_HALO, :] = abuf[ts:ts + CONV_HALO, :]

    abuf[CONV_HALO:CONV_HALO + ts, :] = a
    lag = CONV_HALO - (CONV_K - 1)
    n_zb = w_gb.shape[0]
    per_gate = D_MODEL // ZB_COLS
    for idx in range(n_zb):
        z = _dot(h, w_gb[idx]) + b_gb[idx]
        zbuf[idx] = _sigmoid(z) if idx < N_BRANCH * per_gate else _gelu_tanh(z)
    for rc in range(ts // CONV_ROWS):
        r0 = rc * CONV_ROWS
        for lt in range(CONV_CH // LANES):
            cs = slice(lt * LANES, (lt + 1) * LANES)
            acc = jnp.broadcast_to(conv_b[:, cs], (CONV_ROWS, LANES))
            for r in range(SUBLANES):
                rows = CONV_ROWS + (SUBLANES if r else 0)
                part = None
                for j in range(CONV_K):
                    if (lag + j) % SUBLANES != r:
                        continue
                    off = r0 + (lag + j - r)
                    term = conv_w[j:j + 1, cs] * abuf[off:off + rows, cs]
                    part = term if part is None else part + term
                acc = acc + part[r:r + CONV_ROWS]
            cbuf[r0:r0 + CONV_ROWS, cs] = acc

    def zcols(first, count):
        return jnp.concatenate([zbuf[first + i] for i in range(count)], axis=1)

    for rc in range(ts // CONV_ROWS):
        rs = slice(rc * CONV_ROWS, (rc + 1) * CONV_ROWS)
        y = _layer_norm(cbuf[rs, :], ln_a_g[...], ln_a_b[...])
        a2buf[rs, :] = (y * _sigmoid(y)).astype(BF16)
    y_a = _dot(a2buf[...], w_a_out[...])
    merged = zcols(0, per_gate) * y_a

    zb = zcols(N_BRANCH * per_gate, COL_B // ZB_COLS)
    u = zb[:, :SG_CH]
    v = _layer_norm(zb[:, SG_CH:], ln_b_g[...], ln_b_b[...]).astype(BF16)
    nch = ts // SG_CHUNK
    gw = SG_CH // SG_GROUPS
    ti = lax.broadcasted_iota(jnp.int32, (SG_CHUNK, SG_CHUNK), 0)
    si = lax.broadcasted_iota(jnp.int32, (SG_CHUNK, SG_CHUNK), 1)
    causal = si <= ti
    mixed = []
    for gi in range(SG_GROUPS):
        wsm = jnp.where(causal, ws_ref[gi], 0.0).astype(BF16)
        rhs = jnp.concatenate(
            [v[n * SG_CHUNK:(n + 1) * SG_CHUNK, gi * gw:(gi + 1) * gw] for n in range(nch)], axis=1)
        r = _dot(wsm, rhs)
        mixed.append(jnp.concatenate(
            [r[:, n * gw:(n + 1) * gw] for n in range(nch)], axis=0))
    bs_tile = jnp.concatenate([bs_ref[...]] * nch, axis=0)
    vb = jnp.concatenate(mixed, axis=1) + bs_tile
    y_b = _dot((u * vb).astype(BF16), w_b_out[...])
    merged = merged + zcols(per_gate, per_gate) * y_b

    y_c = _dot(oc_ref[...], w_c_out[...])
    merged = merged + zcols(2 * per_gate, per_gate) * y_c

    y = _dot(merged.astype(BF16), w_mix[...])
    out_ref[...] = x + _rms(y, post_g[...])


def _mixer(x, oc, pre_g, post_g, w_a, w_gb, b_gb, conv_w, conv_b, ln_a_g, ln_a_b, w_a_out,
           ln_b_g, ln_b_b, w_s, bs_cols, w_b_out, w_c_out, w_mix):
    B = x.shape[0]
    ts = SEQ_TILE
    consts = (pre_g, post_g, w_a, w_gb, b_gb, conv_w, conv_b, ln_a_g, ln_a_b, w_a_out,
              ln_b_g, ln_b_b, w_s, bs_cols, w_b_out, w_c_out, w_mix)
    return pl.pallas_call(
        _mixer_kernel,
        out_shape=jax.ShapeDtypeStruct(x.shape, F32),
        grid=(B, SEQ // ts),
        in_specs=[
            pl.BlockSpec((None, ts, D_MODEL), lambda b, s: (b, s, 0)),
            pl.BlockSpec((None, ts, ATT_OUT), lambda b, s: (b, s, 0)),
        ] + [_const_spec(c.shape) for c in consts],
        out_specs=pl.BlockSpec((None, ts, D_MODEL), lambda b, s: (b, s, 0)),
        scratch_shapes=[
            pltpu.VMEM((CONV_HALO + ts, CONV_CH), F32),
            pltpu.VMEM((ts, CONV_CH), F32),
            pltpu.VMEM((ts, CONV_CH), BF16),
            pltpu.VMEM(((N_BRANCH * D_MODEL + COL_B) // ZB_COLS, ts, ZB_COLS), F32),
        ],
        compiler_params=pltpu.CompilerParams(
            dimension_semantics=("arbitrary", "arbitrary"), vmem_limit_bytes=VMEM_LIMIT_BYTES),
        name="parallel_mixer",
    )(x, oc, *consts)


def _xattn_kernel(x_ref, mem_ref, pre_g, post_g, mem_g, w_q, w_kv, w_o, out_ref, k_s, v_s):
    s = pl.program_id(1)

    @pl.when(s == 0)
    def _():
        mn = _rms(mem_ref[...], mem_g[...]).astype(BF16)
        kv = _dot(mn, w_kv[...])
        k_s[...] = kv[:, :D_MODEL].astype(BF16)
        v_s[...] = kv[:, D_MODEL:].astype(BF16)

    heads = [slice(hd * X_HEAD_DIM, (hd + 1) * X_HEAD_DIM) for hd in range(X_HEADS)]
    for sub in range(XATTN_SUBTILES):
        rows = slice(sub * SEQ_TILE, (sub + 1) * SEQ_TILE)
        x = x_ref[rows, :]
        h = _rms(x, pre_g[...]).astype(BF16)
        q = (_dot(h, w_q[...]) * X_HEAD_DIM ** -0.5).astype(BF16)
        scores = [lax.dot_general(q[:, c], k_s[:, c], _NT, preferred_element_type=F32)
                  for c in heads]
        probs = []
        for sc in scores:
            e = jnp.exp(sc - jnp.max(sc, axis=-1, keepdims=True))
            probs.append((e / jnp.sum(e, axis=-1, keepdims=True)).astype(BF16))
        outs = [_dot(p, v_s[:, c]) for p, c in zip(probs, heads)]
        o = jnp.concatenate(outs, axis=1).astype(BF16)
        y = _dot(o, w_o[...])
        out_ref[rows, :] = x + _rms(y, post_g[...])


def _cross_attention(x, mem, pre_g, post_g, mem_g, w_q, w_kv, w_o):
    B = x.shape[0]
    ts = SEQ_TILE * XATTN_SUBTILES
    consts = (pre_g, post_g, mem_g, w_q, w_kv, w_o)
    return pl.pallas_call(
        _xattn_kernel,
        out_shape=jax.ShapeDtypeStruct(x.shape, F32),
        grid=(B, SEQ // ts),
        in_specs=[
            pl.BlockSpec((None, ts, D_MODEL), lambda b, s: (b, s, 0)),
            pl.BlockSpec((None, MEM_LEN, D_MODEL), lambda b, s: (b, 0, 0)),
        ] + [_const_spec(c.shape) for c in consts],
        out_specs=pl.BlockSpec((None, ts, D_MODEL), lambda b, s: (b, s, 0)),
        scratch_shapes=[
            pltpu.VMEM((MEM_LEN, D_MODEL), BF16),
            pltpu.VMEM((MEM_LEN, D_MODEL), BF16),
        ],
        compiler_params=pltpu.CompilerParams(
            dimension_semantics=("arbitrary", "arbitrary"), vmem_limit_bytes=VMEM_LIMIT_BYTES),
        name="memory_cross_attention",
    )(x, mem, *consts)


def _ffn_kernel(x_ref, pre_g, post_g, w_gate, w_val, conv_w, conv_b, w_down, out_ref, gbuf):
    ts = SEQ_TILE
    s = pl.program_id(1)
    x = x_ref[...]
    h = _rms(x, pre_g[...]).astype(BF16)

    @pl.when(s == 0)
    def _():
        gbuf[0:FFN_HALO, :] = jnp.zeros((FFN_HALO, D_FF), F32)

    @pl.when(s > 0)
    def _():
        gbuf[0:FFN_HALO, :] = gbuf[ts:ts + FFN_HALO, :]

    gbuf[FFN_HALO:FFN_HALO + ts, :] = _dot(h, w_gate[...])
    lag = FFN_HALO - (FFN_CONV_K - 1)
    gc = jnp.broadcast_to(conv_b[...], (ts, D_FF))
    for j in range(FFN_CONV_K):
        gc = gc + conv_w[j:j + 1, :] * gbuf[lag + j:lag + j + ts, :]
    act = _gelu_tanh(gc) * _dot(h, w_val[...])
    y = _dot(act.astype(BF16), w_down[...])
    out_ref[...] = x + _rms(y, post_g[...])


def _conv_ffn(x, pre_g, post_g, w_gate, w_val, conv_w, conv_b, w_down):
    B = x.shape[0]
    ts = SEQ_TILE
    consts = (pre_g, post_g, w_gate, w_val, conv_w, conv_b, w_down)
    return pl.pallas_call(
        _ffn_kernel,
        out_shape=jax.ShapeDtypeStruct(x.shape, F32),
        grid=(B, SEQ // ts),
        in_specs=[pl.BlockSpec((None, ts, D_MODEL), lambda b, s: (b, s, 0))]
        + [_const_spec(c.shape) for c in consts],
        out_specs=pl.BlockSpec((None, ts, D_MODEL), lambda b, s: (b, s, 0)),
        scratch_shapes=[pltpu.VMEM((FFN_HALO + ts, D_FF), F32)],
        compiler_params=pltpu.CompilerParams(
            dimension_semantics=("arbitrary", "arbitrary"), vmem_limit_bytes=VMEM_LIMIT_BYTES),
        name="conv_ffn",
    )(x, *consts)


def _row(v):
    return v.reshape(1, -1)


def _column_blocks(w_gate, w_b):
    w = jnp.concatenate([w_gate, w_b], axis=1).astype(BF16)
    return w.reshape(D_MODEL, -1, ZB_COLS).transpose(1, 0, 2)


def _bias_blocks(b_gate):
    b = jnp.concatenate([b_gate.reshape(-1), jnp.zeros((COL_B,), F32)])
    return b.reshape(-1, 1, ZB_COLS)


def _qkv_weights(w_c):
    n = ATT_HEADS * HEAD_DIM
    parts = []
    for gi in range(len(DIL_GROUPS)):
        cols = [w_c[:, t * n + gi * ATT_OUT:t * n + (gi + 1) * ATT_OUT] for t in range(3)]
        parts.append(jnp.concatenate(cols, axis=1))
    return jnp.stack(parts).astype(BF16)


def kernel(x, mem, rel_bias, mix_pre_g, mix_post_g, w_in, b_gate, conv_a_w, conv_a_b, ln_a_g, ln_a_b, w_a_out, ln_b_g, ln_b_b, w_s, b_s, w_b_out, w_c_out, w_mix_out, x_pre_g, x_post_g, mem_g, w_xq, w_xkv, w_xo, ffn_pre_g, ffn_post_g, w_up, conv_f_w, conv_f_b, w_down):
    assert x.shape[1:] == (SEQ, D_MODEL) and mem.shape[1:] == (MEM_LEN, D_MODEL)
    depth = w_in.shape[0]
    bias = _bias_tables(rel_bias)
    gw = SG_CH // SG_GROUPS
    for l in range(depth):
        w = w_in[l]
        oc = _attention(x, _row(mix_pre_g[l]), _qkv_weights(w[:, OFF_C:OFF_G]), bias)
        bs_cols = jnp.repeat(b_s[l].T, gw, axis=1)
        x = _mixer(
            x, oc, _row(mix_pre_g[l]), _row(mix_post_g[l]),
            w[:, :OFF_B].astype(BF16), _column_blocks(w[:, OFF_G:], w[:, OFF_B:OFF_C]),
            _bias_blocks(b_gate[l]), conv_a_w[l], _row(conv_a_b[l]), _row(ln_a_g[l]), _row(ln_a_b[l]),
            w_a_out[l].astype(BF16), _row(ln_b_g[l]), _row(ln_b_b[l]), w_s[l], bs_cols,
            w_b_out[l].astype(BF16), w_c_out[l].astype(BF16), w_mix_out[l].astype(BF16))
        x = _cross_attention(
            x, mem, _row(x_pre_g[l]), _row(x_post_g[l]), _row(mem_g[l]),
            w_xq[l].astype(BF16), w_xkv[l].astype(BF16), w_xo[l].astype(BF16))
        x = _conv_ffn(
            x, _row(ffn_pre_g[l]), _row(ffn_post_g[l]),
            w_up[l][:, :D_FF].astype(BF16), w_up[l][:, D_FF:].astype(BF16),
            conv_f_w[l], _row(conv_f_b[l]), w_down[l].astype(BF16))
    return x
```

```python
import math

import numpy as np
import jax
import jax.numpy as jnp
from jax import lax
from jax.experimental import pallas as pl
from jax.experimental.pallas import tpu as pltpu

D_MODEL = 1024
SEQ = 2048
MEM_LEN = 256
CONV_CH = 512
CONV_K = 31
SG_CH = 512
SG_GROUPS = 4
SG_CHUNK = 128
HEAD_DIM = 64
HEADS_PER_GROUP = 4
DIL_GROUPS = ((128, 1), (512, 4), (2048, 16))
ATT_HEADS = HEADS_PER_GROUP * len(DIL_GROUPS)
ATT_BLOCK = 128
REL_BUCKETS = 32
REL_MAX_DIST = 2048
N_BRANCH = 3
X_HEADS = 4
X_HEAD_DIM = D_MODEL // X_HEADS
D_FF = 2816
FFN_CONV_K = 3
NORM_EPS = 1e-6
LN_EPS = 1e-5

COL_A = 2 * CONV_CH
COL_B = 2 * SG_CH
COL_C = 3 * ATT_HEADS * HEAD_DIM
OFF_B = COL_A
OFF_C = COL_A + COL_B
OFF_G = COL_A + COL_B + COL_C
ATT_OUT = HEADS_PER_GROUP * HEAD_DIM

LANES = 128
SUBLANES = 8
VMEM_LIMIT_BYTES = 56 * 1024 * 1024
ATTN_VMEM_LIMIT_BYTES = 62 * 1024 * 1024

PROJ_TILE = 512
SEQ_TILE = 512
CONV_ROWS = 128
CONV_HALO = 32
FFN_HALO = 8
ZB_COLS = 256
XATTN_SUBTILES = 4
G2_UNROLL = 4
G0_UNROLL = 3

BF16 = jnp.bfloat16
F32 = jnp.float32
_NT = (((1,), (1,)), ((), ()))


def _rms(x, g):
    return x * lax.rsqrt(jnp.mean(x * x, axis=-1, keepdims=True) + NORM_EPS) * g


def _layer_norm(x, g, b):
    mu = jnp.mean(x, axis=-1, keepdims=True)
    xc = x - mu
    var = jnp.mean(xc * xc, axis=-1, keepdims=True)
    return xc * lax.rsqrt(var + LN_EPS) * g + b


def _sigmoid(x):
    return 1.0 / (1.0 + jnp.exp(-x))


def _gelu_tanh(x):
    c = math.sqrt(2.0 / math.pi)
    return 0.5 * x * (1.0 + jnp.tanh(c * (x + 0.044715 * (x * x * x))))


def _dot(a, b):
    return jnp.dot(a, b, preferred_element_type=F32)


def _const_spec(shape):
    n = len(shape)
    return pl.BlockSpec(shape, lambda *_: (0,) * n, pipeline_mode=pl.Buffered(1))


def _bucket_tables():
    qi = np.arange(ATT_BLOCK)[:, None]
    ki = np.arange(2 * ATT_BLOCK)[None, :]
    rel = qi + ATT_BLOCK - ki
    max_exact = REL_BUCKETS // 2
    out = []
    for window, dil in DIL_GROUPS:
        span = window // dil
        dist = np.maximum(rel * dil, 0)
        nf = np.maximum(dist, 1)
        vals = []
        for dt in (np.float32, np.float64):
            x = np.log(nf.astype(dt) / dt(max_exact)) / dt(math.log(REL_MAX_DIST / max_exact))
            vals.append(max_exact + (x * dt(REL_BUCKETS - max_exact)).astype(np.int32))
        valid = (rel >= 0) & (rel <= span)
        assert np.array_equal(vals[0][valid], vals[1][valid])
        large = np.minimum(vals[0], REL_BUCKETS - 1)
        bucket = np.where(dist < max_exact, dist, large)
        out.append(np.where(valid, bucket, -1).astype(np.int32))
    return np.stack(out)


def _bias_kernel(tab_ref, bucket_ref, out_ref):
    for gi in range(len(DIL_GROUPS)):
        bk = bucket_ref[gi]
        for hh in range(HEADS_PER_GROUP):
            h = gi * HEADS_PER_GROUP + hh
            acc = jnp.full(bk.shape, -jnp.inf, F32)
            for b in range(REL_BUCKETS):
                acc = jnp.where(bk == b, tab_ref[b, h], acc)
            out_ref[h] = acc


def _bias_tables(rel_bias):
    buckets = jnp.asarray(_bucket_tables())
    return pl.pallas_call(
        _bias_kernel,
        out_shape=jax.ShapeDtypeStruct((ATT_HEADS, ATT_BLOCK, 2 * ATT_BLOCK), F32),
        in_specs=[pl.BlockSpec(memory_space=pltpu.SMEM),
                  pl.BlockSpec(memory_space=pltpu.VMEM)],
        out_specs=pl.BlockSpec(memory_space=pltpu.VMEM),
        name="rel_bias_tables",
    )(rel_bias, buckets)


def _attend_items(items, q_s, k_s, v_s, bias_ref, m_s, l_s, acc_s, oc_ref):
    lane = lax.broadcasted_iota(jnp.int32, (ATT_BLOCK, LANES), 1)
    lo = lane < HEAD_DIM
    shape = (ATT_BLOCK, LANES)

    scores = []
    for gi, pr, q_row, k_row, nk, _, _ in items:
        cols = slice(pr * LANES, (pr + 1) * LANES)
        q2 = q_s[gi, pl.ds(q_row, ATT_BLOCK), cols]
        k2 = k_s[gi, pl.ds(k_row, nk), cols]
        ha = gi * HEADS_PER_GROUP + 2 * pr
        if nk == ATT_BLOCK:
            ba = bias_ref[ha, :, ATT_BLOCK:]
            bb = bias_ref[ha + 1, :, ATT_BLOCK:]
        else:
            ba = bias_ref[ha]
            bb = bias_ref[ha + 1]
        zero = jnp.zeros_like(q2)
        sa = lax.dot_general(jnp.where(lo, q2, zero), k2, _NT, preferred_element_type=F32) + ba
        sb = lax.dot_general(jnp.where(lo, zero, q2), k2, _NT, preferred_element_type=F32) + bb
        scores.append((sa, sb))

    stats = []
    for pair in scores:
        r = []
        for s in pair:
            m = jnp.max(s, axis=-1, keepdims=True)
            e = jnp.exp(s - m)
            r.append((m, jnp.sum(e, axis=-1, keepdims=True), e.astype(BF16)))
        stats.append(r)

    nums = []
    for (gi, pr, _, k_row, nk, _, _), r in zip(items, stats):
        v2 = v_s[gi, pl.ds(k_row, nk), pr * LANES:(pr + 1) * LANES]
        nums.append(jnp.where(lo, _dot(r[0][2], v2), _dot(r[1][2], v2)))

    for (gi, pr, _, _, _, out_rows, mode), r, a2 in zip(items, stats, nums):
        m2 = jnp.where(lo, jnp.broadcast_to(r[0][0], shape), jnp.broadcast_to(r[1][0], shape))
        l2 = jnp.where(lo, jnp.broadcast_to(r[0][1], shape), jnp.broadcast_to(r[1][1], shape))
        if mode == "init":
            m_s[pr, out_rows, :] = m2
            l_s[pr, out_rows, :] = l2
            acc_s[pr, out_rows, :] = a2
        else:
            mo = m_s[pr, out_rows, :]
            mn = jnp.maximum(mo, m2)
            eo = jnp.exp(mo - mn)
            e2 = jnp.exp(m2 - mn)
            ln = eo * l_s[pr, out_rows, :] + e2 * l2
            an = eo * acc_s[pr, out_rows, :] + e2 * a2
            if mode == "merge":
                m_s[pr, out_rows, :] = mn
                l_s[pr, out_rows, :] = ln
                acc_s[pr, out_rows, :] = an
            else:
                oc_ref[out_rows, pr * LANES:(pr + 1) * LANES] = (an / ln).astype(BF16)


def _attn_kernel(x_ref, g_ref, w_ref, bias_ref, oc_ref,
                 q_s, k_s, v_s, stage, m_s, l_s, acc_s):
    g = g_ref[...]
    hw = ATT_OUT

    def proj_tile(tt, carry):
        r0 = tt * PROJ_TILE
        h = _rms(x_ref[pl.ds(r0, PROJ_TILE), :], g).astype(BF16)
        n_qkv = ATT_HEADS * HEAD_DIM
        for gi, (_, dil) in enumerate(DIL_GROUPS):
            z = jnp.concatenate(
                [_dot(h, w_ref[:, t * n_qkv + gi * hw:t * n_qkv + (gi + 1) * hw]) for t in range(3)],
                axis=1)
            if dil == 1:
                q_s[gi, pl.ds(r0, PROJ_TILE), :] = (z[:, :hw] * HEAD_DIM ** -0.5).astype(BF16)
                k_s[gi, pl.ds(r0, PROJ_TILE), :] = z[:, hw:2 * hw].astype(BF16)
                v_s[gi, pl.ds(r0, PROJ_TILE), :] = z[:, 2 * hw:].astype(BF16)
            else:
                nt = hw // LANES
                for j in range(3 * nt):
                    stage[j] = z[:, j * LANES:(j + 1) * LANES]
                n = PROJ_TILE // dil
                cls_len = SEQ // dil
                for c in range(dil):
                    dst = c * cls_len + tt * n
                    for j in range(3 * nt):
                        zz = stage[j, pl.ds(c, n, stride=dil), :]
                        cols = slice((j % nt) * LANES, (j % nt + 1) * LANES)
                        if j < nt:
                            q_s[gi, pl.ds(dst, n), cols] = (zz * HEAD_DIM ** -0.5).astype(BF16)
                        elif j < 2 * nt:
                            k_s[gi, pl.ds(dst, n), cols] = zz.astype(BF16)
                        else:
                            v_s[gi, pl.ds(dst, n), cols] = zz.astype(BF16)
        return carry

    for tt in range(SEQ // PROJ_TILE):
        proj_tile(tt, 0)

    refs = (q_s, k_s, v_s, bias_ref, m_s, l_s, acc_s, oc_ref)
    pairs = range(HEADS_PER_GROUP // 2)

    dil2 = DIL_GROUPS[2][1]

    def g2_body(i, carry):
        items = []
        for u in range(G2_UNROLL):
            c = i * G2_UNROLL + u
            row = c * ATT_BLOCK
            out_rows = pl.ds(c, ATT_BLOCK, stride=dil2)
            items += [(2, pr, row, row, ATT_BLOCK, out_rows, "init") for pr in pairs]
        _attend_items(items, *refs)
        return carry

    for i in range(dil2 // G2_UNROLL):
        g2_body(i, 0)

    dil1 = DIL_GROUPS[1][1]
    cls1 = SEQ // dil1
    for n in range(cls1 // ATT_BLOCK):
        items = []
        for c in range(dil1):
            row = c * cls1 + n * ATT_BLOCK
            out_rows = pl.ds(n * (ATT_BLOCK * dil1) + c, ATT_BLOCK, stride=dil1)
            if n == 0:
                items += [(1, pr, row, row, ATT_BLOCK, out_rows, "merge") for pr in pairs]
            else:
                items += [(1, pr, row, row - ATT_BLOCK, 2 * ATT_BLOCK, out_rows, "merge")
                          for pr in pairs]
        _attend_items(items, *refs)

    _attend_items([(0, pr, 0, 0, ATT_BLOCK, pl.ds(0, ATT_BLOCK), "final") for pr in pairs], *refs)

    def g0_body(i, carry):
        items = []
        for u in range(G0_UNROLL):
            row = (1 + i * G0_UNROLL + u) * ATT_BLOCK
            prev = row - ATT_BLOCK
            items += [(0, pr, row, prev, 2 * ATT_BLOCK, pl.ds(row, ATT_BLOCK), "final")
                      for pr in pairs]
        _attend_items(items, *refs)
        return carry

    for i in range((SEQ // ATT_BLOCK - 1) // G0_UNROLL):
        g0_body(i, 0)


def _attention(x, pre_g, w_qkv, bias):
    B = x.shape[0]
    ng = len(DIL_GROUPS)
    return pl.pallas_call(
        _attn_kernel,
        out_shape=jax.ShapeDtypeStruct((B, SEQ, ATT_OUT), BF16),
        grid=(B,),
        in_specs=[
            pl.BlockSpec((None, SEQ, D_MODEL), lambda b: (b, 0, 0)),
            _const_spec((1, D_MODEL)),
            _const_spec((D_MODEL, COL_C)),
            _const_spec((ATT_HEADS, ATT_BLOCK, 2 * ATT_BLOCK)),
        ],
        out_specs=pl.BlockSpec((None, SEQ, ATT_OUT), lambda b: (b, 0, 0)),
        scratch_shapes=[
            pltpu.VMEM((ng, SEQ, ATT_OUT), BF16),
            pltpu.VMEM((ng, SEQ, ATT_OUT), BF16),
            pltpu.VMEM((ng, SEQ, ATT_OUT), BF16),
            pltpu.VMEM((3 * ATT_OUT // LANES, PROJ_TILE, LANES), F32),
            pltpu.VMEM((ATT_OUT // LANES, SEQ, LANES), F32),
            pltpu.VMEM((ATT_OUT // LANES, SEQ, LANES), F32),
            pltpu.VMEM((ATT_OUT // LANES, SEQ, LANES), F32),
        ],
        compiler_params=pltpu.CompilerParams(
            dimension_semantics=("arbitrary",), vmem_limit_bytes=ATTN_VMEM_LIMIT_BYTES),
        name="dilated_attention",
    )(x, pre_g, w_qkv, bias)


def _mixer_kernel(x_ref, oc_ref, pre_g, post_g, w_a, w_b, w_g, b_gate,
                  conv_w, conv_b, ln_a_g, ln_a_b, w_a_out,
                  ln_b_g, ln_b_b, ws_ref, bs_ref, w_b_out, w_c_out, w_mix,
                  out_ref, abuf, cbuf, a2buf, zbuf):
    ts = SEQ_TILE
    s = pl.program_id(1)
    x = x_ref[...]
    h = _rms(x, pre_g[...]).astype(BF16)

    za = _dot(h, w_a[...])
    a = za[:, :CONV_CH] * _sigmoid(za[:, CONV_CH:])

    @pl.when(s == 0)
    def _():
        abuf[0:CONV_HALO, :] = jnp.zeros((CONV_HALO, CONV_CH), F32)

    @pl.when(s > 0)
    def _():
        abuf[0:CONV_HALO, :] = abuf[ts:ts + CONV_HALO, :]

    abuf[CONV_HALO:CONV_HALO + ts, :] = a
    lag = CONV_HALO - (CONV_K - 1)
    per_gate = D_MODEL // ZB_COLS
    for idx in range(N_BRANCH * per_gate):
        cols = slice(idx % per_gate * ZB_COLS, (idx % per_gate + 1) * ZB_COLS)
        k = idx // per_gate
        gate_cols = slice(idx * ZB_COLS, (idx + 1) * ZB_COLS)
        zbuf[idx] = _sigmoid(_dot(h, w_g[:, gate_cols]) + b_gate[k:k + 1, cols])
    for i in range(COL_B // ZB_COLS):
        zbuf[N_BRANCH * per_gate + i] = _gelu_tanh(_dot(h, w_b[:, i * ZB_COLS:(i + 1) * ZB_COLS]))
    for rc in range(ts // CONV_ROWS):
        r0 = rc * CONV_ROWS
        for lt in range(CONV_CH // LANES):
            cs = slice(lt * LANES, (lt + 1) * LANES)
            acc = jnp.broadcast_to(conv_b[:, cs], (CONV_ROWS, LANES))
            for r in range(SUBLANES):
                rows = CONV_ROWS + (SUBLANES if r else 0)
                part = None
                for j in range(CONV_K):
                    if (lag + j) % SUBLANES != r:
                        continue
                    off = r0 + (lag + j - r)
                    term = conv_w[j:j + 1, cs] * abuf[off:off + rows, cs]
                    part = term if part is None else part + term
                acc = acc + part[r:r + CONV_ROWS]
            cbuf[r0:r0 + CONV_ROWS, cs] = acc

    def zcols(first, count):
        return jnp.concatenate([zbuf[first + i] for i in range(count)], axis=1)

    for rc in range(ts // CONV_ROWS):
        rs = slice(rc * CONV_ROWS, (rc + 1) * CONV_ROWS)
        y = _layer_norm(cbuf[rs, :], ln_a_g[...], ln_a_b[...])
        a2buf[rs, :] = (y * _sigmoid(y)).astype(BF16)
    y_a = _dot(a2buf[...], w_a_out[...])
    merged = zcols(0, per_gate) * y_a

    zb = zcols(N_BRANCH * per_gate, COL_B // ZB_COLS)
    u = zb[:, :SG_CH]
    v = _layer_norm(zb[:, SG_CH:], ln_b_g[...], ln_b_b[...]).astype(BF16)
    nch = ts // SG_CHUNK
    gw = SG_CH // SG_GROUPS
    ti = lax.broadcasted_iota(jnp.int32, (SG_CHUNK, SG_CHUNK), 0)
    si = lax.broadcasted_iota(jnp.int32, (SG_CHUNK, SG_CHUNK), 1)
    causal = si <= ti
    mixed = []
    for gi in range(SG_GROUPS):
        wsm = jnp.where(causal, ws_ref[gi], 0.0).astype(BF16)
        rhs = jnp.concatenate(
            [v[n * SG_CHUNK:(n + 1) * SG_CHUNK, gi * gw:(gi + 1) * gw] for n in range(nch)], axis=1)
        r = _dot(wsm, rhs)
        mixed.append(jnp.concatenate(
            [r[:, n * gw:(n + 1) * gw] for n in range(nch)], axis=0))
    bs_tile = jnp.concatenate([bs_ref[...]] * nch, axis=0)
    vb = jnp.concatenate(mixed, axis=1) + bs_tile
    y_b = _dot((u * vb).astype(BF16), w_b_out[...])
    merged = merged + zcols(per_gate, per_gate) * y_b

    y_c = _dot(oc_ref[...], w_c_out[...])
    merged = merged + zcols(2 * per_gate, per_gate) * y_c

    y = _dot(merged.astype(BF16), w_mix[...])
    out_ref[...] = x + _rms(y, post_g[...])


def _mixer(x, oc, pre_g, post_g, w_a, w_b, w_g, b_gate, conv_w, conv_b, ln_a_g, ln_a_b, w_a_out,
           ln_b_g, ln_b_b, w_s, bs_cols, w_b_out, w_c_out, w_mix):
    B = x.shape[0]
    ts = SEQ_TILE
    consts = (pre_g, post_g, w_a, w_b, w_g, b_gate, conv_w, conv_b, ln_a_g, ln_a_b, w_a_out,
              ln_b_g, ln_b_b, w_s, bs_cols, w_b_out, w_c_out, w_mix)
    return pl.pallas_call(
        _mixer_kernel,
        out_shape=jax.ShapeDtypeStruct(x.shape, F32),
        grid=(B, SEQ // ts),
        in_specs=[
            pl.BlockSpec((None, ts, D_MODEL), lambda b, s: (b, s, 0)),
            pl.BlockSpec((None, ts, ATT_OUT), lambda b, s: (b, s, 0)),
        ] + [_const_spec(c.shape) for c in consts],
        out_specs=pl.BlockSpec((None, ts, D_MODEL), lambda b, s: (b, s, 0)),
        scratch_shapes=[
            pltpu.VMEM((CONV_HALO + ts, CONV_CH), F32),
            pltpu.VMEM((ts, CONV_CH), F32),
            pltpu.VMEM((ts, CONV_CH), BF16),
            pltpu.VMEM(((N_BRANCH * D_MODEL + COL_B) // ZB_COLS, ts, ZB_COLS), F32),
        ],
        compiler_params=pltpu.CompilerParams(
            dimension_semantics=("arbitrary", "arbitrary"), vmem_limit_bytes=VMEM_LIMIT_BYTES),
        name="parallel_mixer",
    )(x, oc, *consts)


def _xattn_kernel(x_ref, mem_ref, pre_g, post_g, mem_g, w_q, w_kv, w_o, out_ref, k_s, v_s):
    s = pl.program_id(1)

    @pl.when(s == 0)
    def _():
        mn = _rms(mem_ref[...], mem_g[...]).astype(BF16)
        kv = _dot(mn, w_kv[...])
        k_s[...] = kv[:, :D_MODEL].astype(BF16)
        v_s[...] = kv[:, D_MODEL:].astype(BF16)

    heads = [slice(hd * X_HEAD_DIM, (hd + 1) * X_HEAD_DIM) for hd in range(X_HEADS)]
    for sub in range(XATTN_SUBTILES):
        rows = slice(sub * SEQ_TILE, (sub + 1) * SEQ_TILE)
        x = x_ref[rows, :]
        h = _rms(x, pre_g[...]).astype(BF16)
        q = (_dot(h, w_q[...]) * X_HEAD_DIM ** -0.5).astype(BF16)
        scores = [lax.dot_general(q[:, c], k_s[:, c], _NT, preferred_element_type=F32)
                  for c in heads]
        probs = []
        for sc in scores:
            e = jnp.exp(sc - jnp.max(sc, axis=-1, keepdims=True))
            probs.append((e / jnp.sum(e, axis=-1, keepdims=True)).astype(BF16))
        outs = [_dot(p, v_s[:, c]) for p, c in zip(probs, heads)]
        o = jnp.concatenate(outs, axis=1).astype(BF16)
        y = _dot(o, w_o[...])
        out_ref[rows, :] = x + _rms(y, post_g[...])


def _cross_attention(x, mem, pre_g, post_g, mem_g, w_q, w_kv, w_o):
    B = x.shape[0]
    ts = SEQ_TILE * XATTN_SUBTILES
    consts = (pre_g, post_g, mem_g, w_q, w_kv, w_o)
    return pl.pallas_call(
        _xattn_kernel,
        out_shape=jax.ShapeDtypeStruct(x.shape, F32),
        grid=(B, SEQ // ts),
        in_specs=[
            pl.BlockSpec((None, ts, D_MODEL), lambda b, s: (b, s, 0)),
            pl.BlockSpec((None, MEM_LEN, D_MODEL), lambda b, s: (b, 0, 0)),
        ] + [_const_spec(c.shape) for c in consts],
        out_specs=pl.BlockSpec((None, ts, D_MODEL), lambda b, s: (b, s, 0)),
        scratch_shapes=[
            pltpu.VMEM((MEM_LEN, D_MODEL), BF16),
            pltpu.VMEM((MEM_LEN, D_MODEL), BF16),
        ],
        compiler_params=pltpu.CompilerParams(
            dimension_semantics=("arbitrary", "arbitrary"), vmem_limit_bytes=VMEM_LIMIT_BYTES),
        name="memory_cross_attention",
    )(x, mem, *consts)


def _ffn_kernel(x_ref, pre_g, post_g, w_gate, w_val, conv_w, conv_b, w_down, out_ref, gbuf):
    ts = SEQ_TILE
    s = pl.program_id(1)
    x = x_ref[...]
    h = _rms(x, pre_g[...]).astype(BF16)

    @pl.when(s == 0)
    def _():
        gbuf[0:FFN_HALO, :] = jnp.zeros((FFN_HALO, D_FF), F32)

    @pl.when(s > 0)
    def _():
        gbuf[0:FFN_HALO, :] = gbuf[ts:ts + FFN_HALO, :]

    gbuf[FFN_HALO:FFN_HALO + ts, :] = _dot(h, w_gate[...])
    lag = FFN_HALO - (FFN_CONV_K - 1)
    gc = jnp.broadcast_to(conv_b[...], (ts, D_FF))
    for j in range(FFN_CONV_K):
        gc = gc + conv_w[j:j + 1, :] * gbuf[lag + j:lag + j + ts, :]
    act = _gelu_tanh(gc) * _dot(h, w_val[...])
    y = _dot(act.astype(BF16), w_down[...])
    out_ref[...] = x + _rms(y, post_g[...])


def _conv_ffn(x, pre_g, post_g, w_gate, w_val, conv_w, conv_b, w_down):
    B = x.shape[0]
    ts = SEQ_TILE
    consts = (pre_g, post_g, w_gate, w_val, conv_w, conv_b, w_down)
    return pl.pallas_call(
        _ffn_kernel,
        out_shape=jax.ShapeDtypeStruct(x.shape, F32),
        grid=(B, SEQ // ts),
        in_specs=[pl.BlockSpec((None, ts, D_MODEL), lambda b, s: (b, s, 0))]
        + [_const_spec(c.shape) for c in consts],
        out_specs=pl.BlockSpec((None, ts, D_MODEL), lambda b, s: (b, s, 0)),
        scratch_shapes=[pltpu.VMEM((FFN_HALO + ts, D_FF), F32)],
        compiler_params=pltpu.CompilerParams(
            dimension_semantics=("arbitrary", "arbitrary"), vmem_limit_bytes=VMEM_LIMIT_BYTES),
        name="conv_ffn",
    )(x, *consts)


def _row(v):
    return v.reshape(1, -1)


def kernel(x, mem, rel_bias, mix_pre_g, mix_post_g, w_in, b_gate, conv_a_w, conv_a_b, ln_a_g, ln_a_b, w_a_out, ln_b_g, ln_b_b, w_s, b_s, w_b_out, w_c_out, w_mix_out, x_pre_g, x_post_g, mem_g, w_xq, w_xkv, w_xo, ffn_pre_g, ffn_post_g, w_up, conv_f_w, conv_f_b, w_down):
    assert x.shape[1:] == (SEQ, D_MODEL) and mem.shape[1:] == (MEM_LEN, D_MODEL)
    depth = w_in.shape[0]
    bias = _bias_tables(rel_bias)
    gw = SG_CH // SG_GROUPS
    for l in range(depth):
        w = w_in[l]
        oc = _attention(x, _row(mix_pre_g[l]), w[:, OFF_C:OFF_G].astype(BF16), bias)
        bs_cols = jnp.repeat(b_s[l].T, gw, axis=1)
        x = _mixer(
            x, oc, _row(mix_pre_g[l]), _row(mix_post_g[l]),
            w[:, :OFF_B].astype(BF16), w[:, OFF_B:OFF_C].astype(BF16), w[:, OFF_G:].astype(BF16),
            b_gate[l], conv_a_w[l], _row(conv_a_b[l]), _row(ln_a_g[l]), _row(ln_a_b[l]),
            w_a_out[l].astype(BF16), _row(ln_b_g[l]), _row(ln_b_b[l]), w_s[l], bs_cols,
            w_b_out[l].astype(BF16), w_c_out[l].astype(BF16), w_mix_out[l].astype(BF16))
        x = _cross_attention(
            x, mem, _row(x_pre_g[l]), _row(x_post_g[l]), _row(mem_g[l]),
            w_xq[l].astype(BF16), w_xkv[l].astype(BF16), w_xo[l].astype(BF16))
        x = _conv_ffn(
            x, _row(ffn_pre_g[l]), _row(ffn_post_g[l]),
            w_up[l][:, :D_FF].astype(BF16), w_up[l][:, D_FF:].astype(BF16),
            conv_f_w[l], _row(conv_f_b[l]), w_down[l].astype(BF16))
    return x
```
